```python
import math
import jax, jax.numpy as jnp
from jax import lax
import numpy as np

D_MODEL = 4096
BATCH = 2
SEQ = 4096
DEPTH = 4
DEC_BATCH = 16
DEC_SEQ = 64
PAST_LEN = 2048

CHUNK = 64
N_META = 16
POOL_WINDOWS = (2, 4, 8, 16)
POOL_WIDTH = D_MODEL // 4
POOL_GROUP = POOL_WIDTH // len(POOL_WINDOWS)
POOL_HIST = max(POOL_WINDOWS) - 1
HEAD_DIM = 128
ATTN_WIDTH = D_MODEL - POOL_WIDTH
N_HEADS = ATTN_WIDTH // (2 * HEAD_DIM)
MIX_WIDTH = POOL_WIDTH + ATTN_WIDTH
QK_WIDTH = N_HEADS * 2 * HEAD_DIM
IN_WIDTH = POOL_WIDTH + 3 * QK_WIDTH
D_FF = 256 * ((8 * D_MODEL // 3 + 255) // 256)
CONV_W = 3
ROPE_THETA = 10000.0
Q_BLOCK = 128
RMS_EPS = 1e-6
SUBLN_EPS = 1e-5
NEG_INF = -1e30
PAD_CID = 2 ** 30

kernel_name = "hymba_pool_diffattn_convffn_stream"


def rms_norm(x, g, eps):
    xf = x.astype(jnp.float32)
    y = xf * lax.rsqrt(jnp.mean(xf * xf, axis=-1, keepdims=True) + eps)
    return (y * g.astype(jnp.float32)).astype(x.dtype)


def rope(x, pos):
    half = HEAD_DIM // 2
    inv = ROPE_THETA ** (-jnp.arange(half, dtype=jnp.float32) / half)
    ang = pos[:, None] * inv[None, :]
    cos = jnp.cos(ang)[None, :, None, :]
    sin = jnp.sin(ang)[None, :, None, :]
    xf = x.astype(jnp.float32)
    x1, x2 = xf[..., :half], xf[..., half:]
    return jnp.concatenate([x1 * cos - x2 * sin, x2 * cos + x1 * sin], axis=-1).astype(x.dtype)


def pool_mixer(u_full, n_hist, w_pool, pool_scale):
    total = u_full.shape[1]
    uf = u_full.astype(jnp.float32)
    cz = jnp.concatenate([jnp.zeros_like(uf[:, :1]), jnp.cumsum(uf, axis=1)], axis=1)
    t = jnp.arange(n_hist, total)
    hi = cz[:, n_hist + 1:]
    u_cur = uf[:, n_hist:]
    outs = []
    for g, w in enumerate(POOL_WINDOWS):
        sl = slice(g * POOL_GROUP, (g + 1) * POOL_GROUP)
        lo_idx = jnp.maximum(t + 1 - w, 0)
        cnt = jnp.minimum(t + 1, w).astype(jnp.float32)
        mean = (hi[..., sl] - jnp.take(cz[..., sl], lo_idx, axis=1)) / cnt[None, :, None]
        outs.append(jnp.einsum('blc,ce->ble', mean - u_cur[..., sl], w_pool[g].astype(jnp.float32)))
    y = jnp.concatenate(outs, axis=-1) * pool_scale.astype(jnp.float32)
    return y.astype(u_full.dtype)


def diff_attention(q, k, v, q_cid, k_cid, lam):
    B, Lq = q.shape[0], q.shape[1]
    qb = min(Q_BLOCK, Lq)
    nb = -(-Lq // qb)
    pad = nb * qb - Lq
    qp = jnp.pad(q, ((0, 0), (0, pad), (0, 0), (0, 0), (0, 0)))
    cp = jnp.pad(q_cid, (0, pad), constant_values=PAD_CID)
    qs = qp.reshape(B, nb, qb, N_HEADS, 2, HEAD_DIM).transpose(1, 0, 2, 3, 4, 5)
    cs = cp.reshape(nb, qb)
    kf = k.astype(jnp.float32)
    vf = v.astype(jnp.float32)
    scale = HEAD_DIM ** -0.5

    def block(args):
        qblk, cblk = args
        s = jnp.einsum('bqhmd,bkhmd->bmhqk', qblk.astype(jnp.float32), kf) * scale
        mask = k_cid[None, :] <= cblk[:, None]
        s = jnp.where(mask, s, NEG_INF)
        p = jax.nn.softmax(s, axis=-1)
        wts = p[:, 0] - lam * p[:, 1]
        return jnp.einsum('bhqk,bkhe->bqhe', wts, vf)

    o = lax.map(block, (qs, cs))
    o = o.transpose(1, 0, 2, 3, 4).reshape(B, nb * qb, N_HEADS, 2 * HEAD_DIM)[:, :Lq]
    return o.astype(v.dtype)


def trunk_layer(x, pos, q_cid, k_hist, v_hist, hist_cid, pool_hist, conv_hist, lam_init,
                norm_attn, w_in, lambda_q1, lambda_k1, lambda_q2, lambda_k2, subln_g,
                w_pool, pool_scale, w_out, norm_ffn, w_up, conv_w, conv_b, w_down):
    B, L = x.shape[0], x.shape[1]
    h = rms_norm(x, norm_attn, RMS_EPS)
    z = jnp.einsum('bld,de->ble', h, w_in)
    u = z[..., :POOL_WIDTH]
    q = z[..., POOL_WIDTH:POOL_WIDTH + QK_WIDTH].reshape(B, L, 2 * N_HEADS, HEAD_DIM)
    k = z[..., POOL_WIDTH + QK_WIDTH:POOL_WIDTH + 2 * QK_WIDTH].reshape(B, L, 2 * N_HEADS, HEAD_DIM)
    v = z[..., POOL_WIDTH + 2 * QK_WIDTH:].reshape(B, L, N_HEADS, 2 * HEAD_DIM)
    q = rope(q, pos).reshape(B, L, N_HEADS, 2, HEAD_DIM)
    k_rows = rope(k, pos).reshape(B, L, N_HEADS, 2 * HEAD_DIM)
    k_all = jnp.concatenate([k_hist, k_rows], axis=1)
    v_all = jnp.concatenate([v_hist, v], axis=1)
    Lk = k_all.shape[1]
    k_cid = jnp.concatenate([hist_cid, q_cid])
    lam = (jnp.exp(jnp.sum(lambda_q1.astype(jnp.float32) * lambda_k1.astype(jnp.float32)))
           - jnp.exp(jnp.sum(lambda_q2.astype(jnp.float32) * lambda_k2.astype(jnp.float32)))
           + lam_init)
    o = diff_attention(q, k_all.reshape(B, Lk, N_HEADS, 2, HEAD_DIM), v_all, q_cid, k_cid, lam)
    o = rms_norm(o, subln_g, SUBLN_EPS) * (1.0 - lam_init)
    u_full = jnp.concatenate([pool_hist, u], axis=1)
    y_pool = pool_mixer(u_full, pool_hist.shape[1], w_pool, pool_scale)
    mix = jnp.concatenate([y_pool, o.reshape(B, L, ATTN_WIDTH)], axis=-1)
    x = x + jnp.einsum('ble,ed->bld', mix, w_out)
    h2 = rms_norm(x, norm_ffn, RMS_EPS)
    up = jnp.einsum('bld,df->blf', h2, w_up)
    up_full = jnp.concatenate([conv_hist, up], axis=1)
    c = conv_b + conv_w[0] * up_full[:, 0:L]
    for j in range(1, CONV_W):
        c = c + conv_w[j] * up_full[:, j:j + L]
    gate, val = c[..., :D_FF], c[..., D_FF:]
    x = x + jnp.einsum('blf,fd->bld', jax.nn.silu(gate) * val, w_down)
    return x, k_rows, v, u_full[:, -POOL_HIST:], up_full[:, -(CONV_W - 1):]


def setup_inputs(seed: int = 0) -> dict:
    key = jax.random.key(seed)
    ks = jax.random.split(key, 24)
    f32 = jnp.float32
    nrm = lambda k, shape, s: jax.random.normal(k, shape, f32) * s
    return {
        "x_prompt": nrm(ks[0], (BATCH, SEQ, D_MODEL), 1.0),
        "x_sample": nrm(ks[1], (DEC_BATCH, DEC_SEQ, D_MODEL), 1.0),
        "cache_k": nrm(ks[2], (DEPTH, DEC_BATCH, PAST_LEN, N_HEADS, 2 * HEAD_DIM), 1.0),
        "cache_v": nrm(ks[3], (DEPTH, DEC_BATCH, PAST_LEN, N_HEADS, 2 * HEAD_DIM), 1.0),
        "state_pool": nrm(ks[4], (DEPTH, DEC_BATCH, POOL_HIST, POOL_WIDTH), 1.0),
        "state_ffn_conv": nrm(ks[5], (DEPTH, DEC_BATCH, CONV_W - 1, 2 * D_FF), 1.0),
        "meta_tokens": nrm(ks[6], (N_META, D_MODEL), 1.0),
        "norm_attn": 1.0 + nrm(ks[7], (DEPTH, D_MODEL), 0.02),
        "w_in": nrm(ks[8], (DEPTH, D_MODEL, IN_WIDTH), D_MODEL ** -0.5),
        "lambda_q1": nrm(ks[9], (DEPTH, HEAD_DIM), 0.1),
        "lambda_k1": nrm(ks[10], (DEPTH, HEAD_DIM), 0.1),
        "lambda_q2": nrm(ks[11], (DEPTH, HEAD_DIM), 0.1),
        "lambda_k2": nrm(ks[12], (DEPTH, HEAD_DIM), 0.1),
        "subln_g": 1.0 + nrm(ks[13], (DEPTH, 2 * HEAD_DIM), 0.02),
        "w_pool": nrm(ks[14], (DEPTH, len(POOL_WINDOWS), POOL_GROUP, POOL_GROUP), POOL_GROUP ** -0.5),
        "pool_scale": 1.0 + nrm(ks[15], (DEPTH, POOL_WIDTH), 0.1),
        "w_out": nrm(ks[16], (DEPTH, MIX_WIDTH, D_MODEL), MIX_WIDTH ** -0.5),
        "norm_ffn": 1.0 + nrm(ks[17], (DEPTH, D_MODEL), 0.02),
        "w_up": nrm(ks[18], (DEPTH, D_MODEL, 2 * D_FF), D_MODEL ** -0.5),
        "conv_w": nrm(ks[19], (DEPTH, CONV_W, 2 * D_FF), CONV_W ** -0.5),
        "conv_b": nrm(ks[20], (DEPTH, 2 * D_FF), 0.01),
        "w_down": nrm(ks[21], (DEPTH, D_FF, D_MODEL), D_FF ** -0.5),
        "final_norm": 1.0 + nrm(ks[22], (D_MODEL,), 0.02),
    }


def reference(x_prompt, x_sample, cache_k, cache_v, state_pool, state_ffn_conv, meta_tokens,
              norm_attn, w_in, lambda_q1, lambda_k1, lambda_q2, lambda_k2, subln_g,
              w_pool, pool_scale, w_out, norm_ffn, w_up, conv_w, conv_b, w_down, final_norm):
    dt = x_prompt.dtype
    Bp = x_prompt.shape[0]
    T = N_META + x_prompt.shape[1]
    Ls = x_sample.shape[1]
    xp = jnp.concatenate([jnp.broadcast_to(meta_tokens.astype(dt)[None], (Bp, N_META, D_MODEL)), x_prompt], axis=1)
    tp = jnp.arange(T, dtype=jnp.int32)
    pos_p = tp.astype(jnp.float32)
    cid_p = (tp - N_META) // CHUNK
    empty_kv = jnp.zeros((Bp, 0, N_HEADS, 2 * HEAD_DIM), dt)
    empty_pool = jnp.zeros((Bp, 0, POOL_WIDTH), dt)
    zero_conv = jnp.zeros((Bp, CONV_W - 1, 2 * D_FF), dt)
    empty_cid = jnp.zeros((0,), jnp.int32)
    xs = x_sample
    ts = PAST_LEN + jnp.arange(Ls, dtype=jnp.int32)
    pos_s = ts.astype(jnp.float32)
    cid_s = ts // CHUNK
    cid_hist = jnp.arange(PAST_LEN, dtype=jnp.int32) // CHUNK

    kp_l, vp_l, pp_l, cp_l = [], [], [], []
    ks_l, vs_l, ps_l, cs_l = [], [], [], []
    for l in range(DEPTH):
        lam_init = 0.8 - 0.6 * math.exp(-0.3 * l)
        lw = (norm_attn[l], w_in[l], lambda_q1[l], lambda_k1[l], lambda_q2[l], lambda_k2[l], subln_g[l],
              w_pool[l], pool_scale[l], w_out[l], norm_ffn[l], w_up[l], conv_w[l], conv_b[l], w_down[l])
        xp, kp, vp, pp, cp = trunk_layer(xp, pos_p, cid_p, empty_kv, empty_kv, empty_cid,
                                         empty_pool, zero_conv, lam_init, *lw)
        xs, ksn, vsn, psn, csn = trunk_layer(xs, pos_s, cid_s, cache_k[l], cache_v[l], cid_hist,
                                             state_pool[l], state_ffn_conv[l], lam_init, *lw)
        kp_l.append(kp); vp_l.append(vp); pp_l.append(pp); cp_l.append(cp)
        ks_l.append(ksn); vs_l.append(vsn); ps_l.append(psn); cs_l.append(csn)

    y_prompt = rms_norm(xp[:, N_META:], final_norm, RMS_EPS)
    y_sample = rms_norm(xs, final_norm, RMS_EPS)
    k_prompt = jnp.stack(kp_l)
    v_prompt = jnp.stack(vp_l)
    pool_prompt = jnp.stack(pp_l)
    conv_prompt = jnp.stack(cp_l)
    k_sample = jnp.stack(ks_l)
    v_sample = jnp.stack(vs_l)
    pool_sample = jnp.stack(ps_l)
    conv_sample = jnp.stack(cs_l)
    return (y_prompt, y_sample, k_prompt, v_prompt, pool_prompt, conv_prompt,
            k_sample, v_sample, pool_sample, conv_sample)
```

```python
import functools
import math

import jax
import jax.numpy as jnp
from jax import lax
from jax.experimental import pallas as pl
from jax.experimental.pallas import tpu as pltpu

F32 = jnp.float32
BF16 = jnp.bfloat16

D_MODEL = 4096
CHUNK = 64
N_META = 16
POOL_WINDOWS = (2, 4, 8, 16)
POOL_WIDTH = D_MODEL // 4
POOL_GROUP = POOL_WIDTH // len(POOL_WINDOWS)
POOL_HIST = max(POOL_WINDOWS) - 1
HEAD_DIM = 128
HEAD_WIDTH = 2 * HEAD_DIM
ATTN_WIDTH = D_MODEL - POOL_WIDTH
N_HEADS = ATTN_WIDTH // HEAD_WIDTH
D_FF = 256 * ((8 * D_MODEL // 3 + 255) // 256)
CONV_W = 3
ROPE_THETA = 10000.0
RMS_EPS = 1e-6
SUBLN_EPS = 1e-5
NEG_INF = -1e30

SUBLANES = 8
VMEM_LIMIT = 56 * 1024 * 1024

TM = 1024
TM_DOWN = 512
TN_IN = 512
TN_OUT = 512
TN_FF = 256
TN_DOWN = 256
TQ = 256


def _params(*sem):
    return pltpu.CompilerParams(dimension_semantics=sem, vmem_limit_bytes=VMEM_LIMIT)


def _dot(a, b):
    return jnp.dot(a, b, preferred_element_type=F32)


def _dot_nt(a, b):
    return lax.dot_general(a, b, (((1,), (1,)), ((), ())), preferred_element_type=F32)


def _rmsnorm_rows(x, g, eps):
    ms = jnp.mean(x * x, axis=-1, keepdims=True)
    return x * lax.rsqrt(ms + eps) * g


def _rmsnorm_kernel(x_ref, g_ref, o_ref, *, n_full, tail):
    i = pl.program_id(0)

    @pl.when(i < n_full)
    def _():
        o_ref[...] = _rmsnorm_rows(x_ref[...], g_ref[...], RMS_EPS).astype(o_ref.dtype)

    if tail:
        @pl.when(i == n_full)
        def _():
            o_ref[:tail] = _rmsnorm_rows(x_ref[:tail], g_ref[...], RMS_EPS).astype(o_ref.dtype)


def _rmsnorm(x, g, out_dtype, *, row0=0, rows=None, tm=TM_DOWN):
    rows = x.shape[0] - row0 if rows is None else rows
    n_full, tail = divmod(rows, tm)
    off = row0 // tm
    assert row0 % tm == 0
    return pl.pallas_call(
        functools.partial(_rmsnorm_kernel, n_full=n_full, tail=tail),
        grid=(n_full + (1 if tail else 0),),
        in_specs=[pl.BlockSpec((tm, D_MODEL), lambda i: (i + off, 0)),
                  pl.BlockSpec((1, D_MODEL), lambda i: (0, 0))],
        out_specs=pl.BlockSpec((tm, D_MODEL), lambda i: (i, 0)),
        out_shape=jax.ShapeDtypeStruct((rows, D_MODEL), out_dtype),
        compiler_params=_params("arbitrary"),
        name="rmsnorm",
    )(x, g.reshape(1, D_MODEL))


def _inproj_kernel(x_ref, w_ref, cos_ref, sin_ref, *o_refs, mode, tn, n_full, tail):
    i = pl.program_id(0)

    def compute(rows):
        acc = _dot(x_ref[rows, :], w_ref[...])
        if mode == "u":
            o_refs[0][rows, :] = acc
            return
        if mode == "v":
            o_refs[0][rows, :] = acc
            o_refs[1][rows, :] = acc.astype(BF16)
            return
        cos = cos_ref[rows, :]
        sin = sin_ref[rows, :]
        for c in range(tn // HEAD_DIM):
            cols = slice(c * HEAD_DIM, (c + 1) * HEAD_DIM)
            blk = acc[:, cols]
            rot = blk * cos + pltpu.roll(blk, HEAD_DIM // 2, axis=1) * sin
            if mode == "q":
                o_refs[0][rows, cols] = (rot * (HEAD_DIM ** -0.5)).astype(BF16)
            else:
                o_refs[0][rows, cols] = rot
                o_refs[1][rows, cols] = rot.astype(BF16)

    @pl.when(i < n_full)
    def _():
        compute(slice(None))

    if tail:
        @pl.when(i == n_full)
        def _():
            compute(slice(0, tail))


def _inproj(h, w, cos, sin, *, col0, ncols, mode, tm=TM, tn=TN_IN):
    rows = h.shape[0]
    n_full, tail = divmod(rows, tm)
    joff = col0 // tn
    out_dtypes = {"u": (F32,), "q": (BF16,), "k": (F32, BF16), "v": (F32, BF16)}[mode]
    return pl.pallas_call(
        functools.partial(_inproj_kernel, mode=mode, tn=tn, n_full=n_full, tail=tail),
        grid=(n_full + (1 if tail else 0), ncols // tn),
        in_specs=[pl.BlockSpec((tm, D_MODEL), lambda i, j: (i, 0)),
                  pl.BlockSpec((D_MODEL, tn), lambda i, j: (0, j + joff)),
                  pl.BlockSpec((tm, HEAD_DIM), lambda i, j: (i, 0)),
                  pl.BlockSpec((tm, HEAD_DIM), lambda i, j: (i, 0))],
        out_specs=[pl.BlockSpec((tm, tn), lambda i, j: (i, j)) for _ in out_dtypes],
        out_shape=[jax.ShapeDtypeStruct((rows, ncols), dt) for dt in out_dtypes],
        compiler_params=_params("arbitrary", "arbitrary"),
        name="inproj_" + mode,
    )(h, w, cos, sin)


def _pool_kernel(*refs, rows, hist_rows):
    if hist_rows:
        u_ref, hist_ref, w_ref, scale_ref, o_ref, buf_ref = refs
        buf_ref[N_META - hist_rows:N_META, :] = hist_ref[...]
    else:
        u_ref, w_ref, scale_ref, o_ref, buf_ref = refs
        buf_ref[:N_META, :] = jnp.zeros((N_META, POOL_WIDTH), F32)
    buf_ref[N_META:, :] = u_ref[...]
    for g, win in enumerate(POOL_WINDOWS):
        cols = slice(g * POOL_GROUP, (g + 1) * POOL_GROUP)
        cur = buf_ref[N_META:, cols]
        wsum = cur
        for back in range(1, win):
            wsum = wsum + buf_ref[N_META - back:N_META - back + rows, cols]
        if hist_rows:
            mean = wsum * (1.0 / win)
        else:
            t = lax.broadcasted_iota(jnp.int32, (rows, 1), 0)
            mean = wsum / jnp.minimum(t + 1, win).astype(F32)
        y = _dot((mean - cur).astype(BF16), w_ref[g])
        o_ref[:, cols] = (y * scale_ref[:, cols]).astype(BF16)


def _pool_call(u, hist, w_pool, scale, *, grid, rows, u_map, hist_spec, out_rows, out_map, name):
    hist_rows = 0 if hist is None else hist_spec.block_shape[-2]
    in_specs = [pl.BlockSpec((rows, POOL_WIDTH), u_map)]
    args = [u]
    if hist is not None:
        in_specs.append(hist_spec)
        args.append(hist)
    nd = len(grid)
    in_specs += [pl.BlockSpec((len(POOL_WINDOWS), POOL_GROUP, POOL_GROUP), lambda *_: (0, 0, 0)),
                 pl.BlockSpec((1, POOL_WIDTH), lambda *_: (0, 0))]
    args += [w_pool, scale.reshape(1, POOL_WIDTH)]
    return pl.pallas_call(
        functools.partial(_pool_kernel, rows=rows, hist_rows=hist_rows),
        grid=grid,
        in_specs=in_specs,
        out_specs=pl.BlockSpec((rows, POOL_WIDTH), out_map),
        out_shape=jax.ShapeDtypeStruct((out_rows, POOL_WIDTH), BF16),
        scratch_shapes=[pltpu.VMEM((N_META + rows, POOL_WIDTH), F32)],
        compiler_params=_params(*(("arbitrary",) * nd)),
        name=name,
    )(*args)


def _lambda(lq1_ref, lk1_ref, lq2_ref, lk2_ref, lam_init):
    a = jnp.sum(lq1_ref[...] * lk1_ref[...], axis=-1, keepdims=True)
    b = jnp.sum(lq2_ref[...] * lk2_ref[...], axis=-1, keepdims=True)
    return jnp.exp(a) - jnp.exp(b) + lam_init


def _attn_finish(o1, l1, o2, l2, lam, g, lam_init):
    o = o1 / l1 - lam * (o2 / l2)
    ms = jnp.mean(o * o, axis=-1, keepdims=True)
    return (o * lax.rsqrt(ms + SUBLN_EPS) * g * (1.0 - lam_init)).astype(BF16)


def _attn_tok_kernel(q_ref, k_ref, v_ref, km_ref, vm_ref, lq1_ref, lk1_ref, lq2_ref, lk2_ref, g_ref,
                     o_ref, m_ref, l_ref, acc_ref, *, tq, lam_init):
    i = pl.program_id(2)
    qs = (q_ref[:, :HEAD_DIM], q_ref[:, HEAD_DIM:])

    vm = vm_ref[...]
    for m in range(2):
        s = _dot_nt(qs[m], km_ref[:, m * HEAD_DIM:(m + 1) * HEAD_DIM])
        mx = jnp.max(s, axis=-1, keepdims=True)
        p = jnp.exp(s - mx)
        m_ref[m] = mx
        l_ref[m] = jnp.sum(p, axis=-1, keepdims=True)
        acc_ref[m] = _dot(p.astype(BF16), vm)

    def step(j, mask):
        start = pl.multiple_of(j * tq, tq)
        k = k_ref[pl.ds(start, tq), :]
        v = v_ref[pl.ds(start, tq), :]
        for m in range(2):
            s = _dot_nt(qs[m], k[:, m * HEAD_DIM:(m + 1) * HEAD_DIM])
            if mask is not None:
                s = jnp.where(mask, s, NEG_INF)
            m_old = m_ref[m]
            m_new = jnp.maximum(m_old, jnp.max(s, axis=-1, keepdims=True))
            alpha = jnp.exp(m_old - m_new)
            p = jnp.exp(s - m_new)
            l_ref[m] = alpha * l_ref[m] + jnp.sum(p, axis=-1, keepdims=True)
            acc_ref[m] = alpha * acc_ref[m] + _dot(p.astype(BF16), v)
            m_ref[m] = m_new

    def body(j, carry):
        step(j, None)
        return carry

    lax.fori_loop(0, i, body, 0)
    qc = lax.broadcasted_iota(jnp.int32, (tq, tq), 0) // CHUNK
    kc = lax.broadcasted_iota(jnp.int32, (tq, tq), 1) // CHUNK
    step(i, kc <= qc)

    lam = _lambda(lq1_ref, lk1_ref, lq2_ref, lk2_ref, lam_init)
    o_ref[...] = _attn_finish(acc_ref[0], l_ref[0], acc_ref[1], l_ref[1], lam, g_ref[...], lam_init)


def _attn_full_kernel(*refs, has_cache, lam_init):
    if has_cache:
        q_ref, kc_ref, vc_ref, kn_ref, vn_ref, lq1_ref, lk1_ref, lq2_ref, lk2_ref, g_ref, o_ref = refs
        vc = vc_ref[...].astype(BF16)
    else:
        q_ref, kn_ref, vn_ref, lq1_ref, lk1_ref, lq2_ref, lk2_ref, g_ref, o_ref = refs
    vn = vn_ref[...]
    outs = []
    for m in range(2):
        cols = slice(m * HEAD_DIM, (m + 1) * HEAD_DIM)
        q = q_ref[:, cols]
        s_n = _dot_nt(q, kn_ref[:, cols])
        mx = jnp.max(s_n, axis=-1, keepdims=True)
        if has_cache:
            s_c = _dot_nt(q, kc_ref[:, cols].astype(BF16))
            mx = jnp.maximum(mx, jnp.max(s_c, axis=-1, keepdims=True))
        p_n = jnp.exp(s_n - mx)
        l = jnp.sum(p_n, axis=-1, keepdims=True)
        acc = _dot(p_n.astype(BF16), vn)
        if has_cache:
            p_c = jnp.exp(s_c - mx)
            l = l + jnp.sum(p_c, axis=-1, keepdims=True)
            acc = acc + _dot(p_c.astype(BF16), vc)
        outs.append((acc, l))
    lam = _lambda(lq1_ref, lk1_ref, lq2_ref, lk2_ref, lam_init)
    o_ref[...] = _attn_finish(outs[0][0], outs[0][1], outs[1][0], outs[1][1], lam, g_ref[...], lam_init)


def _small_specs(nd):
    vec = pl.BlockSpec((1, HEAD_DIM), lambda *_: (0, 0))
    return [vec, vec, vec, vec, pl.BlockSpec((1, HEAD_WIDTH), lambda *_: (0, 0))]


def _attention(q, k, v, cache_k, cache_v, lams, g, lam_init, *, n_batch, seq, n_dec, dec_len, layer):
    n_tok = n_batch * seq
    row_smp = n_tok
    row_meta = n_tok + n_dec * dec_len
    small = [a.reshape(1, HEAD_DIM) for a in lams] + [g.reshape(1, HEAD_WIDTH)]
    tq = TQ
    nq = seq // tq
    meta_blk = row_meta // N_META

    o_tok = pl.pallas_call(
        functools.partial(_attn_tok_kernel, tq=tq, lam_init=lam_init),
        grid=(n_batch, N_HEADS, nq),
        in_specs=[pl.BlockSpec((tq, HEAD_WIDTH), lambda b, h, i: (b * nq + i, h)),
                  pl.BlockSpec((seq, HEAD_WIDTH), lambda b, h, i: (b, h)),
                  pl.BlockSpec((seq, HEAD_WIDTH), lambda b, h, i: (b, h)),
                  pl.BlockSpec((N_META, HEAD_WIDTH), lambda b, h, i: (meta_blk, h)),
                  pl.BlockSpec((N_META, HEAD_WIDTH), lambda b, h, i: (meta_blk, h))] + _small_specs(3),
        out_specs=pl.BlockSpec((tq, HEAD_WIDTH), lambda b, h, i: (b * nq + i, h)),
        out_shape=jax.ShapeDtypeStruct((n_tok, ATTN_WIDTH), BF16),
        scratch_shapes=[pltpu.VMEM((2, tq, 1), F32), pltpu.VMEM((2, tq, 1), F32),
                        pltpu.VMEM((2, tq, HEAD_WIDTH), F32)],
        compiler_params=_params("arbitrary", "arbitrary", "arbitrary"),
        name="attn_prompt",
    )(q, k, v, k, v, *small)

    past = cache_k.shape[2]
    smp_blk = row_smp // dec_len
    o_smp = pl.pallas_call(
        functools.partial(_attn_full_kernel, has_cache=True, lam_init=lam_init),
        grid=(n_dec, N_HEADS),
        in_specs=[pl.BlockSpec((dec_len, HEAD_WIDTH), lambda b, h: (smp_blk + b, h)),
                  pl.BlockSpec((None, None, past, HEAD_WIDTH), lambda b, h: (layer, b, 0, h)),
                  pl.BlockSpec((None, None, past, HEAD_WIDTH), lambda b, h: (layer, b, 0, h)),
                  pl.BlockSpec((dec_len, HEAD_WIDTH), lambda b, h: (smp_blk + b, h)),
                  pl.BlockSpec((dec_len, HEAD_WIDTH), lambda b, h: (smp_blk + b, h))] + _small_specs(2),
        out_specs=pl.BlockSpec((dec_len, HEAD_WIDTH), lambda b, h: (b, h)),
        out_shape=jax.ShapeDtypeStruct((n_dec * dec_len, ATTN_WIDTH), BF16),
        compiler_params=_params("arbitrary", "arbitrary"),
        name="attn_sample",
    )(q, cache_k, cache_v, k, v, *small)

    o_meta = pl.pallas_call(
        functools.partial(_attn_full_kernel, has_cache=False, lam_init=lam_init),
        grid=(N_HEADS,),
        in_specs=[pl.BlockSpec((N_META, HEAD_WIDTH), lambda h: (meta_blk, h)),
                  pl.BlockSpec((N_META, HEAD_WIDTH), lambda h: (meta_blk, h)),
                  pl.BlockSpec((N_META, HEAD_WIDTH), lambda h: (meta_blk, h))] + _small_specs(1),
        out_specs=pl.BlockSpec((N_META, HEAD_WIDTH), lambda h: (0, h)),
        out_shape=jax.ShapeDtypeStruct((N_META, ATTN_WIDTH), BF16),
        compiler_params=_params("arbitrary"),
        name="attn_meta",
    )(q, k, v, *small)
    return o_tok, o_smp, o_meta


def _mm_res_kernel(a_ref, w_ref, r_ref, o_ref, *, n_full, tail):
    i = pl.program_id(0)

    @pl.when(i < n_full)
    def _():
        o_ref[...] = r_ref[...] + _dot(a_ref[...], w_ref[...])

    if tail:
        @pl.when(i == n_full)
        def _():
            o_ref[:tail] = r_ref[:tail] + _dot(a_ref[:tail], w_ref[...])


def _mm_res(a, w, res, *, tm, tn, name):
    rows, kdim = a.shape
    ncols = w.shape[1]
    n_full, tail = divmod(rows, tm)
    return pl.pallas_call(
        functools.partial(_mm_res_kernel, n_full=n_full, tail=tail),
        grid=(n_full + (1 if tail else 0), ncols // tn),
        in_specs=[pl.BlockSpec((tm, kdim), lambda i, j: (i, 0)),
                  pl.BlockSpec((kdim, tn), lambda i, j: (0, j)),
                  pl.BlockSpec((tm, tn), lambda i, j: (i, j))],
        out_specs=pl.BlockSpec((tm, tn), lambda i, j: (i, j)),
        out_shape=jax.ShapeDtypeStruct((rows, ncols), F32),
        compiler_params=_params("arbitrary", "arbitrary"),
        name=name,
    )(a, w, res)


def _silu_gate(cg, cv):
    return cg * (1.0 / (1.0 + jnp.exp(-cg))) * cv


def _up_conv_kernel(h_ref, wg_ref, wv_ref, cwg_ref, cwv_ref, cbg_ref, cbv_ref, stg_ref, stv_ref,
                    act_ref, cs_ref, sg_ref, sv_ref, carry_ref, mcarry_ref,
                    *, tm, n_tok_tiles, tiles_per_seq, n_batch, n_dec, dec_len, n_j):
    i = pl.program_id(0)
    j = pl.program_id(1)
    w_refs = (wg_ref, wv_ref)
    cw_refs = (cwg_ref, cwv_ref)
    cb_refs = (cbg_ref, cbv_ref)
    st_refs = (stg_ref, stv_ref)
    s_refs = (sg_ref, sv_ref)

    def conv(part, up, n):
        s_ref = s_refs[part]
        cw = cw_refs[part]
        s_ref[SUBLANES:SUBLANES + n, :] = up
        p1 = s_ref[SUBLANES - 1:SUBLANES - 1 + n, :]
        p2 = s_ref[SUBLANES - 2:SUBLANES - 2 + n, :]
        return cb_refs[part][...] + cw[0:1, :] * p2 + cw[1:2, :] * p1 + cw[2:3, :] * up

    @pl.when(i == 0)
    def _meta():
        h = h_ref[:N_META, :]
        c = []
        for part in range(2):
            up = _dot(h, w_refs[part][...])
            s_refs[part][:SUBLANES, :] = jnp.zeros((SUBLANES, up.shape[1]), F32)
            c.append(conv(part, up, N_META))
            mcarry_ref[part, j] = up[N_META - SUBLANES:, :]
            cs_ref[part * n_j + j] = jnp.zeros(cs_ref.shape[1:], F32)
        act_ref[:N_META, :] = _silu_gate(c[0], c[1]).astype(BF16)

    @pl.when((i >= 1) & (i <= n_tok_tiles))
    def _prompt():
        p = i - 1
        first = (p % tiles_per_seq) == 0
        h = h_ref[...]
        c = []
        ups = []
        for part in range(2):
            up = _dot(h, w_refs[part][...])

            @pl.when(first)
            def _():
                s_refs[part][:SUBLANES, :] = mcarry_ref[part, j]

            @pl.when(jnp.logical_not(first))
            def _():
                s_refs[part][:SUBLANES, :] = carry_ref[part, j]

            c.append(conv(part, up, tm))
            carry_ref[part, j] = up[tm - SUBLANES:, :]
            ups.append(up)
        act_ref[...] = _silu_gate(c[0], c[1]).astype(BF16)
        for b in range(n_batch):
            @pl.when(p == (b + 1) * tiles_per_seq - 1)
            def _():
                for part in range(2):
                    cs_ref[part * n_j + j, 2 * b:2 * b + 2, :] = ups[part][tm - 2:, :]

    @pl.when(i == n_tok_tiles + 1)
    def _sample():
        h = h_ref[...]
        ups = [_dot(h, w_refs[part][...]) for part in range(2)]
        for s in range(n_dec):
            rows = slice(s * dec_len, (s + 1) * dec_len)
            c = []
            for part in range(2):
                up = ups[part][rows, :]
                s_refs[part][SUBLANES - 2:SUBLANES, :] = st_refs[part][s]
                c.append(conv(part, up, dec_len))
                r = 2 * (n_batch + s)
                cs_ref[part * n_j + j, r:r + 2, :] = up[dec_len - 2:, :]
            act_ref[rows, :] = _silu_gate(c[0], c[1]).astype(BF16)


def _up_conv(h2, w_up, conv_w, conv_b, state, *, n_batch, seq, n_dec, dec_len, tm=TM, tn=TN_FF):
    rows = h2.shape[0]
    n_tok_tiles = n_batch * seq // tm
    assert n_dec * dec_len == tm and seq % tm == 0
    n_i = n_tok_tiles + 2
    n_j = D_FF // tn
    n_seq = n_batch + n_dec
    cs_rows = -(-2 * n_seq // SUBLANES) * SUBLANES

    def row_map(i, j):
        return ((i + n_i - 1) % n_i, 0)

    def gcol(i, j):
        return (0, j)

    def vcol(i, j):
        return (0, j + n_j)

    act, cs = pl.pallas_call(
        functools.partial(_up_conv_kernel, tm=tm, n_tok_tiles=n_tok_tiles, tiles_per_seq=seq // tm,
                          n_batch=n_batch, n_dec=n_dec, dec_len=dec_len, n_j=n_j),
        grid=(n_i, n_j),
        in_specs=[pl.BlockSpec((tm, D_MODEL), row_map),
                  pl.BlockSpec((D_MODEL, tn), gcol), pl.BlockSpec((D_MODEL, tn), vcol),
                  pl.BlockSpec((CONV_W, tn), gcol), pl.BlockSpec((CONV_W, tn), vcol),
                  pl.BlockSpec((1, tn), gcol), pl.BlockSpec((1, tn), vcol),
                  pl.BlockSpec((n_dec, CONV_W - 1, tn), lambda i, j: (0, 0, j)),
                  pl.BlockSpec((n_dec, CONV_W - 1, tn), lambda i, j: (0, 0, j + n_j))],
        out_specs=[pl.BlockSpec((tm, tn), lambda i, j: ((i + n_i - 1) % n_i, j)),
                   pl.BlockSpec((2 * n_j, cs_rows, tn), lambda i, j: (0, 0, 0))],
        out_shape=[jax.ShapeDtypeStruct((rows, D_FF), BF16),
                   jax.ShapeDtypeStruct((2 * n_j, cs_rows, tn), F32)],
        scratch_shapes=[pltpu.VMEM((SUBLANES + tm, tn), F32), pltpu.VMEM((SUBLANES + tm, tn), F32),
                        pltpu.VMEM((2, n_j, SUBLANES, tn), F32), pltpu.VMEM((2, n_j, SUBLANES, tn), F32)],
        compiler_params=_params("arbitrary", "arbitrary"),
        name="up_conv",
    )(h2, w_up, w_up, conv_w, conv_w, conv_b.reshape(1, 2 * D_FF), conv_b.reshape(1, 2 * D_FF), state, state)
    cs = cs[:, :2 * n_seq, :].reshape(2 * n_j, n_seq, 2, tn).transpose(1, 2, 0, 3).reshape(n_seq, 2, 2 * D_FF)
    return act, cs


def _rope_tables(pos):
    half = HEAD_DIM // 2
    inv = ROPE_THETA ** (-jnp.arange(half, dtype=F32) / half)
    ang = pos[:, None] * inv[None, :]
    cos = jnp.cos(ang)
    sin = jnp.sin(ang)
    return jnp.concatenate([cos, cos], axis=-1), jnp.concatenate([-sin, sin], axis=-1)


def kernel(x_prompt, x_sample, cache_k, cache_v, state_pool, state_ffn_conv, meta_tokens, norm_attn, w_in,
           lambda_q1, lambda_k1, lambda_q2, lambda_k2, subln_g, w_pool, pool_scale, w_out, norm_ffn, w_up,
           conv_w, conv_b, w_down, final_norm):
    n_batch, seq, _ = x_prompt.shape
    n_dec, dec_len, _ = x_sample.shape
    depth = w_in.shape[0]
    past = cache_k.shape[2]
    n_tok = n_batch * seq
    n_smp = n_dec * dec_len
    row_smp, row_meta = n_tok, n_tok + n_smp
    assert dec_len == CHUNK and past % CHUNK == 0 and seq % TM == 0 and n_smp % TM == 0

    x = jnp.concatenate([x_prompt.reshape(n_tok, D_MODEL), x_sample.reshape(n_smp, D_MODEL),
                         meta_tokens.astype(x_prompt.dtype)], axis=0)
    pos = jnp.concatenate([jnp.tile(N_META + jnp.arange(seq), n_batch),
                           jnp.tile(past + jnp.arange(dec_len), n_dec),
                           jnp.arange(N_META)]).astype(F32)
    cos, sin = _rope_tables(pos)

    w_in_b, w_pool_b, w_out_b = w_in.astype(BF16), w_pool.astype(BF16), w_out.astype(BF16)
    w_up_b, w_down_b = w_up.astype(BF16), w_down.astype(BF16)
    cache_k2 = cache_k.reshape(depth, n_dec, past, ATTN_WIDTH)
    cache_v2 = cache_v.reshape(depth, n_dec, past, ATTN_WIDTH)

    tmp = TM_DOWN
    tiles = seq // tmp
    meta_blk = row_meta // N_META
    outs = {n: [] for n in ("kp", "vp", "pp", "cp", "ks", "vs", "ps", "cs")}
    for l in range(depth):
        lam_init = 0.8 - 0.6 * math.exp(-0.3 * l)
        h = _rmsnorm(x, norm_attn[l], BF16)
        (u,) = _inproj(h, w_in_b[l], cos, sin, col0=0, ncols=POOL_WIDTH, mode="u")
        (q,) = _inproj(h, w_in_b[l], cos, sin, col0=POOL_WIDTH, ncols=ATTN_WIDTH, mode="q")
        k32, k = _inproj(h, w_in_b[l], cos, sin, col0=POOL_WIDTH + ATTN_WIDTH, ncols=ATTN_WIDTH, mode="k")
        v32, v = _inproj(h, w_in_b[l], cos, sin, col0=POOL_WIDTH + 2 * ATTN_WIDTH, ncols=ATTN_WIDTH, mode="v")

        y_tok = _pool_call(
            u, u, w_pool_b[l], pool_scale[l], grid=(n_batch, tiles), rows=tmp,
            u_map=lambda b, t: (b * tiles + t, 0),
            hist_spec=pl.BlockSpec(
                (N_META, POOL_WIDTH),
                lambda b, t: (jnp.where(t == 0, meta_blk, (b * seq + t * tmp) // N_META - 1), 0)),
            out_rows=n_tok, out_map=lambda b, t: (b * tiles + t, 0), name="pool_prompt")
        y_smp = _pool_call(
            u, state_pool[l], w_pool_b[l], pool_scale[l], grid=(n_dec,), rows=dec_len,
            u_map=lambda b: (row_smp // dec_len + b, 0),
            hist_spec=pl.BlockSpec((None, POOL_HIST, POOL_WIDTH), lambda b: (b, 0, 0)),
            out_rows=n_smp, out_map=lambda b: (b, 0), name="pool_sample")
        y_meta = _pool_call(
            u, None, w_pool_b[l], pool_scale[l], grid=(1,), rows=N_META,
            u_map=lambda b: (meta_blk, 0), hist_spec=None,
            out_rows=N_META, out_map=lambda b: (0, 0), name="pool_meta")

        o_tok, o_smp, o_meta = _attention(
            q, k, v, cache_k2, cache_v2, (lambda_q1[l], lambda_k1[l], lambda_q2[l], lambda_k2[l]), subln_g[l],
            lam_init, n_batch=n_batch, seq=seq, n_dec=n_dec, dec_len=dec_len, layer=l)

        mix = jnp.concatenate([jnp.concatenate([y_tok, o_tok], axis=1),
                               jnp.concatenate([y_smp, o_smp], axis=1),
                               jnp.concatenate([y_meta, o_meta], axis=1)], axis=0)
        x = _mm_res(mix, w_out_b[l], x, tm=TM, tn=TN_OUT, name="out_proj")
        h2 = _rmsnorm(x, norm_ffn[l], BF16)
        act, cs = _up_conv(h2, w_up_b[l], conv_w[l], conv_b[l], state_ffn_conv[l],
                           n_batch=n_batch, seq=seq, n_dec=n_dec, dec_len=dec_len)
        x = _mm_res(act, w_down_b[l], x, tm=TM_DOWN, tn=TN_DOWN, name="down_proj")

        def prompt_rows(a32):
            width = a32.shape[1]
            meta = jnp.broadcast_to(a32[row_meta:][None], (n_batch, N_META, width))
            return jnp.concatenate([meta, a32[:n_tok].reshape(n_batch, seq, width)], axis=1)

        outs["kp"].append(prompt_rows(k32).reshape(n_batch, N_META + seq, N_HEADS, HEAD_WIDTH))
        outs["vp"].append(prompt_rows(v32).reshape(n_batch, N_META + seq, N_HEADS, HEAD_WIDTH))
        outs["ks"].append(k32[row_smp:row_meta].reshape(n_dec, dec_len, N_HEADS, HEAD_WIDTH))
        outs["vs"].append(v32[row_smp:row_meta].reshape(n_dec, dec_len, N_HEADS, HEAD_WIDTH))
        outs["pp"].append(u[:n_tok].reshape(n_batch, seq, POOL_WIDTH)[:, seq - POOL_HIST:])
        outs["ps"].append(u[row_smp:row_meta].reshape(n_dec, dec_len, POOL_WIDTH)[:, dec_len - POOL_HIST:])
        outs["cp"].append(cs[:n_batch])
        outs["cs"].append(cs[n_batch:])

    y_prompt = _rmsnorm(x, final_norm, F32, row0=0, rows=n_tok).reshape(n_batch, seq, D_MODEL)
    y_sample = _rmsnorm(x, final_norm, F32, row0=row_smp, rows=n_smp).reshape(n_dec, dec_len, D_MODEL)
    st = {n: jnp.stack(vs) for n, vs in outs.items()}
    return (y_prompt, y_sample, st["kp"], st["vp"], st["pp"], st["cp"],
            st["ks"], st["vs"], st["ps"], st["cs"])
```

```python
import functools
import math

import jax
import jax.numpy as jnp
from jax import lax
from jax.experimental import pallas as pl
from jax.experimental.pallas import tpu as pltpu

F32 = jnp.float32
BF16 = jnp.bfloat16

D_MODEL = 4096
CHUNK = 64
N_META = 16
POOL_WINDOWS = (2, 4, 8, 16)
POOL_WIDTH = D_MODEL // 4
POOL_GROUP = POOL_WIDTH // len(POOL_WINDOWS)
POOL_HIST = max(POOL_WINDOWS) - 1
HEAD_DIM = 128
HEAD_WIDTH = 2 * HEAD_DIM
ATTN_WIDTH = D_MODEL - POOL_WIDTH
N_HEADS = ATTN_WIDTH // HEAD_WIDTH
D_FF = 256 * ((8 * D_MODEL // 3 + 255) // 256)
CONV_W = 3
ROPE_THETA = 10000.0
RMS_EPS = 1e-6
SUBLN_EPS = 1e-5
NEG_INF = -1e30
LOG2E = math.log2(math.e)

LANES = 128
SUBLANES = 8
VMEM_LIMIT = 56 * 1024 * 1024

TM = 1024
TM_DOWN = 512
TN_IN = 512
TN_OUT = 512
TN_FF = 256
TN_DOWN = 256
TQ = 512


def _params(*sem):
    return pltpu.CompilerParams(dimension_semantics=sem, vmem_limit_bytes=VMEM_LIMIT)


def _dot(a, b):
    return jnp.dot(a, b, preferred_element_type=F32)


def _dot_nt(a, b):
    return lax.dot_general(a, b, (((1,), (1,)), ((), ())), preferred_element_type=F32)


def _rmsnorm_rows(x, g, eps):
    ms = jnp.mean(x * x, axis=-1, keepdims=True)
    return x * lax.rsqrt(ms + eps) * g


def _rmsnorm_kernel(x_ref, g_ref, o_ref, *, n_full, tail):
    i = pl.program_id(0)

    @pl.when(i < n_full)
    def _():
        o_ref[...] = _rmsnorm_rows(x_ref[...], g_ref[...], RMS_EPS).astype(o_ref.dtype)

    if tail:
        @pl.when(i == n_full)
        def _():
            o_ref[:tail] = _rmsnorm_rows(x_ref[:tail], g_ref[...], RMS_EPS).astype(o_ref.dtype)


def _rmsnorm(x, g, out_dtype, *, row0=0, rows=None, tm=TM_DOWN):
    rows = x.shape[0] - row0 if rows is None else rows
    n_full, tail = divmod(rows, tm)
    off = row0 // tm
    assert row0 % tm == 0
    return pl.pallas_call(
        functools.partial(_rmsnorm_kernel, n_full=n_full, tail=tail),
        grid=(n_full + (1 if tail else 0),),
        in_specs=[pl.BlockSpec((tm, D_MODEL), lambda i: (i + off, 0)),
                  pl.BlockSpec((1, D_MODEL), lambda i: (0, 0))],
        out_specs=pl.BlockSpec((tm, D_MODEL), lambda i: (i, 0)),
        out_shape=jax.ShapeDtypeStruct((rows, D_MODEL), out_dtype),
        compiler_params=_params("arbitrary"),
        name="rmsnorm",
    )(x, g.reshape(1, D_MODEL))


def _inproj_kernel(x_ref, w_ref, cos_ref, sin_ref, *o_refs, mode, f32_out, tn, n_full, tail):
    i = pl.program_id(0)
    if f32_out != "rows":
        o_refs = o_refs[-2:]

    def store32(rows, cols, val):
        head, off = divmod(cols.start, HEAD_WIDTH)
        lanes = slice(off, off + cols.stop - cols.start)
        if f32_out == "heads":
            o_refs[0][head, rows, lanes] = val
        elif f32_out == "meta_heads":
            for b in range(o_refs[0].shape[0]):
                o_refs[0][b, head, rows, lanes] = val
        else:
            o_refs[0][rows, cols] = val

    def compute(rows):
        acc = _dot(x_ref[rows, :], w_ref[...])
        if mode == "u":
            o_refs[0][rows, :] = acc
            return
        if mode == "v":
            for c in range(tn // HEAD_WIDTH):
                cols = slice(c * HEAD_WIDTH, (c + 1) * HEAD_WIDTH)
                store32(rows, cols, acc[:, cols])
            o_refs[1][rows, :] = acc.astype(BF16)
            return
        cos = cos_ref[rows, :]
        sin = sin_ref[rows, :]
        for c in range(tn // HEAD_DIM):
            cols = slice(c * HEAD_DIM, (c + 1) * HEAD_DIM)
            blk = acc[:, cols]
            rot = blk * cos + pltpu.roll(blk, HEAD_DIM // 2, axis=1) * sin
            if mode == "q":
                o_refs[0][rows, cols] = (rot * (LOG2E * HEAD_DIM ** -0.5)).astype(BF16)
            else:
                store32(rows, cols, rot)
                o_refs[1][rows, cols] = rot.astype(BF16)

    @pl.when(i < n_full)
    def _():
        compute(slice(None))

    if tail:
        @pl.when(i == n_full)
        def _():
            compute(slice(0, tail))


def _inproj(h, w, cos, sin, layer, *, col0, ncols, mode, row0=0, rows=None, tm=TM, tn=TN_IN,
            f32_out="rows", cache_out=None, cache_shape=None, seq=None):
    rows = h.shape[0] - row0 if rows is None else rows
    n_full, tail = divmod(rows, tm)
    ioff = row0 // tm
    joff = col0 // tn
    assert row0 % tm == 0 and col0 % tn == 0
    out_dtypes = {"u": (F32,), "q": (BF16,), "k": (F32, BF16), "v": (F32, BF16)}[mode]
    row_spec = pl.BlockSpec((tm, tn), lambda i, j: (i, j))
    out_specs = [row_spec for _ in out_dtypes]
    out_shape = [jax.ShapeDtypeStruct((rows, ncols), dt) for dt in out_dtypes]
    in_specs = [pl.BlockSpec((tm, D_MODEL), lambda i, j: (i + ioff, 0)),
                pl.BlockSpec((None, D_MODEL, tn), lambda i, j: (layer, 0, j + joff)),
                pl.BlockSpec((tm, HEAD_DIM), lambda i, j: (i + ioff, 0)),
                pl.BlockSpec((tm, HEAD_DIM), lambda i, j: (i + ioff, 0))]
    args = [h, w, cos, sin]
    aliases = {}
    if f32_out != "rows":
        hpt = tn // HEAD_WIDTH
        n_batch = cache_shape[1]
        assert tail == 0
        if f32_out == "heads":
            tps = seq // tm
            out_specs[0] = pl.BlockSpec(
                (None, None, pl.Element(hpt), pl.Element(tm), pl.Element(HEAD_WIDTH)),
                lambda i, j: (layer, i // tps, hpt * j, pl.multiple_of(N_META + (i % tps) * tm, N_META), 0))
        else:
            assert rows == tm == N_META
            out_specs[0] = pl.BlockSpec(
                (None, pl.Element(n_batch), pl.Element(hpt), pl.Element(N_META), pl.Element(HEAD_WIDTH)),
                lambda i, j: (layer, 0, hpt * j, 0, 0))
        out_shape[0] = jax.ShapeDtypeStruct(cache_shape, F32)
        if cache_out is not None:
            in_specs.append(pl.BlockSpec(memory_space=pl.ANY))
            args.append(cache_out)
            aliases = {len(args) - 1: 0}
    return pl.pallas_call(
        functools.partial(_inproj_kernel, mode=mode, f32_out=f32_out, tn=tn, n_full=n_full, tail=tail),
        grid=(n_full + (1 if tail else 0), ncols // tn),
        in_specs=in_specs,
        out_specs=out_specs,
        out_shape=out_shape,
        input_output_aliases=aliases,
        compiler_params=_params("arbitrary", "arbitrary"),
        name="inproj_" + mode,
    )(*args)


def _pool_kernel(*refs, rows, hist_rows):
    if hist_rows:
        u_ref, hist_ref, w_ref, scale_ref, o_ref, buf_ref = refs
        buf_ref[N_META - hist_rows:N_META, :] = hist_ref[...]
    else:
        u_ref, w_ref, scale_ref, o_ref, buf_ref = refs
        buf_ref[:N_META, :] = jnp.zeros((N_META, POOL_WIDTH), F32)
    buf_ref[N_META:, :] = u_ref[...]
    for g, win in enumerate(POOL_WINDOWS):
        cols = slice(g * POOL_GROUP, (g + 1) * POOL_GROUP)
        cur = buf_ref[N_META:, cols]
        wsum = cur
        for back in range(1, win):
            wsum = wsum + buf_ref[N_META - back:N_META - back + rows, cols]
        if hist_rows:
            mean = wsum * (1.0 / win)
        else:
            t = lax.broadcasted_iota(jnp.int32, (rows, 1), 0)
            mean = wsum / jnp.minimum(t + 1, win).astype(F32)
        y = _dot((mean - cur).astype(BF16), w_ref[g])
        o_ref[:, cols] = (y * scale_ref[:, cols]).astype(BF16)


def _pool_call(u, hist, w_pool, scale, layer, *, grid, rows, u_map, hist_spec, out_rows, out_map, name):
    hist_rows = 0 if hist is None else hist_spec.block_shape[-2]
    in_specs = [pl.BlockSpec((rows, POOL_WIDTH), u_map)]
    args = [u]
    if hist is not None:
        in_specs.append(hist_spec)
        args.append(hist)
    nd = len(grid)
    in_specs += [pl.BlockSpec((None, len(POOL_WINDOWS), POOL_GROUP, POOL_GROUP), lambda *_: (layer, 0, 0, 0)),
                 pl.BlockSpec((None, 1, POOL_WIDTH), lambda *_: (layer, 0, 0))]
    args += [w_pool, scale]
    return pl.pallas_call(
        functools.partial(_pool_kernel, rows=rows, hist_rows=hist_rows),
        grid=grid,
        in_specs=in_specs,
        out_specs=pl.BlockSpec((rows, POOL_WIDTH), out_map),
        out_shape=jax.ShapeDtypeStruct((out_rows, POOL_WIDTH), BF16),
        scratch_shapes=[pltpu.VMEM((N_META + rows, POOL_WIDTH), F32)],
        compiler_params=_params(*(("arbitrary",) * nd)),
        name=name,
    )(*args)


def _lambda(lq1_ref, lk1_ref, lq2_ref, lk2_ref, lam_init):
    a = jnp.sum(lq1_ref[...] * lk1_ref[...], axis=-1, keepdims=True)
    b = jnp.sum(lq2_ref[...] * lk2_ref[...], axis=-1, keepdims=True)
    return jnp.exp(a) - jnp.exp(b) + lam_init


def _attn_finish(o1, l1, o2, l2, lam, g, lam_init):
    o = o1 / l1 - lam * (o2 / l2)
    ms = jnp.mean(o * o, axis=-1, keepdims=True)
    return (o * lax.rsqrt(ms + SUBLN_EPS) * g * (1.0 - lam_init)).astype(BF16)


def _lane_blocks(x):
    return [x[:, c * LANES:(c + 1) * LANES] for c in range(x.shape[1] // LANES)]


def _attn_chain(q, k, v, m_ref, l_ref, acc_ref, m, rows, mask):
    s = _dot_nt(q, k)
    if mask is not None:
        s = jnp.where(mask, s, NEG_INF)
    blocks = _lane_blocks(s)
    mx = functools.reduce(jnp.maximum, blocks)
    m_old = m_ref[m, rows, :]
    m_new = jnp.maximum(m_old, jnp.max(mx, axis=-1, keepdims=True))
    alpha = jnp.exp2(m_old - m_new)
    ps = [jnp.exp2(b - m_new) for b in blocks]
    l_ref[m, rows, :] = alpha * l_ref[m, rows, :] + functools.reduce(jnp.add, ps)
    pv = _dot(jnp.concatenate([p.astype(BF16) for p in ps], axis=1), v)
    acc_ref[m, rows, :] = jnp.concatenate([alpha, alpha], axis=1) * acc_ref[m, rows, :] + pv
    m_ref[m, rows, :] = m_new


def _attn_tok_kernel(q_ref, k_ref, v_ref, km_ref, vm_ref, lq1_ref, lk1_ref, lq2_ref, lk2_ref, g_ref,
                     o_ref, m_ref, l_ref, acc_ref, *, tq, lam_init):
    i = pl.program_id(2)
    half = tq // 2
    groups = (slice(0, half), slice(half, tq))
    maps = (slice(0, HEAD_DIM), slice(HEAD_DIM, HEAD_WIDTH))

    lane = lax.broadcasted_iota(jnp.int32, (half, LANES), 1)
    vm = vm_ref[...]
    for rows in groups:
        for m, cols in enumerate(maps):
            s = _dot_nt(q_ref[rows, cols], km_ref[:, cols])
            mx = jnp.max(s, axis=-1, keepdims=True)
            p = jnp.exp2(s - mx)
            m_ref[m, rows, :] = jnp.broadcast_to(mx, (half, LANES))
            l_ref[m, rows, :] = jnp.where(lane == 0, jnp.sum(p, axis=-1, keepdims=True), 0.0)
            acc_ref[m, rows, :] = _dot(p.astype(BF16), vm)

    def update(rows, start, size, mask):
        keys = pl.ds(start, size)
        for m, cols in enumerate(maps):
            _attn_chain(q_ref[rows, cols], k_ref[keys, cols], v_ref[keys, :], m_ref, l_ref, acc_ref, m, rows, mask)

    def body(j, carry):
        start = pl.multiple_of(j * tq, tq)
        for rows in groups:
            update(rows, start, tq, None)
        return carry

    lax.fori_loop(0, i, body, 0)

    qc = lax.broadcasted_iota(jnp.int32, (half, half), 0) // CHUNK
    kc = lax.broadcasted_iota(jnp.int32, (half, half), 1) // CHUNK
    mask = kc <= qc
    d0 = pl.multiple_of(i * tq, tq)
    d1 = pl.multiple_of(i * tq + half, half)
    update(groups[0], d0, half, mask)
    update(groups[1], d0, half, None)
    update(groups[1], d1, half, mask)

    lam = _lambda(lq1_ref, lk1_ref, lq2_ref, lk2_ref, lam_init)
    for rows in groups:
        l1 = jnp.sum(l_ref[0, rows, :], axis=-1, keepdims=True)
        l2 = jnp.sum(l_ref[1, rows, :], axis=-1, keepdims=True)
        o_ref[rows, :] = _attn_finish(acc_ref[0, rows, :], l1, acc_ref[1, rows, :], l2, lam, g_ref[...], lam_init)


def _attn_full_kernel(*refs, has_cache, lam_init):
    if has_cache:
        q_ref, kc_ref, vc_ref, kn_ref, vn_ref, lq1_ref, lk1_ref, lq2_ref, lk2_ref, g_ref, o_ref = refs
        vc = vc_ref[...].astype(BF16)
    else:
        q_ref, kn_ref, vn_ref, lq1_ref, lk1_ref, lq2_ref, lk2_ref, g_ref, o_ref = refs
    vn = vn_ref[...]
    outs = []
    for m in range(2):
        cols = slice(m * HEAD_DIM, (m + 1) * HEAD_DIM)
        q = q_ref[:, cols]
        s_n = _dot_nt(q, kn_ref[:, cols])
        mx = jnp.max(s_n, axis=-1, keepdims=True)
        if has_cache:
            s_c = _dot_nt(q, kc_ref[:, cols].astype(BF16))
            mx = jnp.maximum(mx, jnp.max(s_c, axis=-1, keepdims=True))
        p_n = jnp.exp2(s_n - mx)
        l = jnp.sum(p_n, axis=-1, keepdims=True)
        acc = _dot(p_n.astype(BF16), vn)
        if has_cache:
            p_c = jnp.exp2(s_c - mx)
            l = l + jnp.sum(p_c, axis=-1, keepdims=True)
            acc = acc + _dot(p_c.astype(BF16), vc)
        outs.append((acc, l))
    lam = _lambda(lq1_ref, lk1_ref, lq2_ref, lk2_ref, lam_init)
    o_ref[...] = _attn_finish(outs[0][0], outs[0][1], outs[1][0], outs[1][1], lam, g_ref[...], lam_init)


def _small_specs(layer):
    vec = pl.BlockSpec((None, 1, HEAD_DIM), lambda *_: (layer, 0, 0))
    return [vec, vec, vec, vec, pl.BlockSpec((None, 1, HEAD_WIDTH), lambda *_: (layer, 0, 0))]


def _attention(q, k_tok, v_tok, k_smp, v_smp, k_meta, v_meta, cache_k, cache_v, small, lam_init, layer,
               *, n_batch, seq, n_dec, dec_len):
    n_tok = n_batch * seq
    n_smp = n_dec * dec_len
    tq = TQ
    nq = seq // tq
    q_meta_blk = (n_tok + n_smp) // N_META

    o_tok = pl.pallas_call(
        functools.partial(_attn_tok_kernel, tq=tq, lam_init=lam_init),
        grid=(n_batch, N_HEADS, nq),
        in_specs=[pl.BlockSpec((tq, HEAD_WIDTH), lambda b, h, i: (b * nq + i, h)),
                  pl.BlockSpec((seq, HEAD_WIDTH), lambda b, h, i: (b, h)),
                  pl.BlockSpec((seq, HEAD_WIDTH), lambda b, h, i: (b, h)),
                  pl.BlockSpec((N_META, HEAD_WIDTH), lambda b, h, i: (0, h)),
                  pl.BlockSpec((N_META, HEAD_WIDTH), lambda b, h, i: (0, h))] + _small_specs(layer),
        out_specs=pl.BlockSpec((tq, HEAD_WIDTH), lambda b, h, i: (b * nq + i, h)),
        out_shape=jax.ShapeDtypeStruct((n_tok, ATTN_WIDTH), BF16),
        scratch_shapes=[pltpu.VMEM((2, tq, LANES), F32), pltpu.VMEM((2, tq, LANES), F32),
                        pltpu.VMEM((2, tq, HEAD_WIDTH), F32)],
        compiler_params=_params("arbitrary", "arbitrary", "arbitrary"),
        name="attn_prompt",
    )(q, k_tok, v_tok, k_meta, v_meta, *small)

    past = cache_k.shape[3]
    smp_blk = n_tok // dec_len
    o_smp = pl.pallas_call(
        functools.partial(_attn_full_kernel, has_cache=True, lam_init=lam_init),
        grid=(n_dec, N_HEADS),
        in_specs=[pl.BlockSpec((dec_len, HEAD_WIDTH), lambda b, h: (smp_blk + b, h)),
                  pl.BlockSpec((None, None, None, past, HEAD_WIDTH), lambda b, h: (layer, b, h, 0, 0)),
                  pl.BlockSpec((None, None, None, past, HEAD_WIDTH), lambda b, h: (layer, b, h, 0, 0)),
                  pl.BlockSpec((dec_len, HEAD_WIDTH), lambda b, h: (b, h)),
                  pl.BlockSpec((dec_len, HEAD_WIDTH), lambda b, h: (b, h))] + _small_specs(layer),
        out_specs=pl.BlockSpec((dec_len, HEAD_WIDTH), lambda b, h: (b, h)),
        out_shape=jax.ShapeDtypeStruct((n_smp, ATTN_WIDTH), BF16),
        compiler_params=_params("arbitrary", "arbitrary"),
        name="attn_sample",
    )(q, cache_k, cache_v, k_smp, v_smp, *small)

    o_meta = pl.pallas_call(
        functools.partial(_attn_full_kernel, has_cache=False, lam_init=lam_init),
        grid=(N_HEADS,),
        in_specs=[pl.BlockSpec((N_META, HEAD_WIDTH), lambda h: (q_meta_blk, h)),
                  pl.BlockSpec((N_META, HEAD_WIDTH), lambda h: (0, h)),
                  pl.BlockSpec((N_META, HEAD_WIDTH), lambda h: (0, h))] + _small_specs(layer),
        out_specs=pl.BlockSpec((N_META, HEAD_WIDTH), lambda h: (0, h)),
        out_shape=jax.ShapeDtypeStruct((N_META, ATTN_WIDTH), BF16),
        compiler_params=_params("arbitrary"),
        name="attn_meta",
    )(q, k_meta, v_meta, *small)
    return o_tok, o_smp, o_meta


def _out_proj_kernel(yt_ref, ot_ref, ys_ref, os_ref, ym_ref, om_ref, w_ref, r_ref, o_ref,
                     *, n_tok_tiles, n_smp_tiles):
    i = pl.program_id(0)

    def compute(y_ref, a_ref, rows):
        o_ref[rows, :] = (r_ref[rows, :] + _dot(y_ref[rows, :], w_ref[:POOL_WIDTH, :])
                          + _dot(a_ref[rows, :], w_ref[POOL_WIDTH:, :]))

    @pl.when(i < n_tok_tiles)
    def _():
        compute(yt_ref, ot_ref, slice(None))

    @pl.when((i >= n_tok_tiles) & (i < n_tok_tiles + n_smp_tiles))
    def _():
        compute(ys_ref, os_ref, slice(None))

    @pl.when(i == n_tok_tiles + n_smp_tiles)
    def _():
        compute(ym_ref, om_ref, slice(0, N_META))


def _out_proj(y_parts, o_parts, w_out, res, layer, *, tm=TM, tn=TN_OUT):
    rows = res.shape[0]
    n_tok_tiles = y_parts[0].shape[0] // tm
    n_smp_tiles = y_parts[1].shape[0] // tm
    n_i = n_tok_tiles + n_smp_tiles + 1
    assert rows == (n_i - 1) * tm + N_META

    def tok_map(i, j):
        return (jnp.minimum(i, n_tok_tiles - 1), 0)

    def smp_map(i, j):
        return (jnp.clip(i - n_tok_tiles, 0, n_smp_tiles - 1), 0)

    def meta_map(i, j):
        return (0, 0)

    in_specs = []
    args = []
    for part, (y, o) in enumerate(zip(y_parts, o_parts)):
        row_map = (tok_map, smp_map, meta_map)[part]
        blk = N_META if part == 2 else tm
        in_specs += [pl.BlockSpec((blk, POOL_WIDTH), row_map), pl.BlockSpec((blk, ATTN_WIDTH), row_map)]
        args += [y, o]
    in_specs += [pl.BlockSpec((None, D_MODEL, tn), lambda i, j: (layer, 0, j)),
                 pl.BlockSpec((tm, tn), lambda i, j: (i, j))]
    return pl.pallas_call(
        functools.partial(_out_proj_kernel, n_tok_tiles=n_tok_tiles, n_smp_tiles=n_smp_tiles),
        grid=(n_i, D_MODEL // tn),
        in_specs=in_specs,
        out_specs=pl.BlockSpec((tm, tn), lambda i, j: (i, j)),
        out_shape=jax.ShapeDtypeStruct((rows, D_MODEL), F32),
        compiler_params=_params("arbitrary", "arbitrary"),
        name="out_proj",
    )(*args, w_out, res)


def _mm_res_kernel(a_ref, w_ref, r_ref, o_ref, *, n_full, tail):
    i = pl.program_id(0)

    @pl.when(i < n_full)
    def _():
        o_ref[...] = r_ref[...] + _dot(a_ref[...], w_ref[...])

    if tail:
        @pl.when(i == n_full)
        def _():
            o_ref[:tail] = r_ref[:tail] + _dot(a_ref[:tail], w_ref[...])


def _mm_res(a, w, res, layer, *, tm, tn, name):
    rows, kdim = a.shape
    ncols = w.shape[2]
    n_full, tail = divmod(rows, tm)
    return pl.pallas_call(
        functools.partial(_mm_res_kernel, n_full=n_full, tail=tail),
        grid=(n_full + (1 if tail else 0), ncols // tn),
        in_specs=[pl.BlockSpec((tm, kdim), lambda i, j: (i, 0)),
                  pl.BlockSpec((None, kdim, tn), lambda i, j: (layer, 0, j)),
                  pl.BlockSpec((tm, tn), lambda i, j: (i, j))],
        out_specs=pl.BlockSpec((tm, tn), lambda i, j: (i, j)),
        out_shape=jax.ShapeDtypeStruct((rows, ncols), F32),
        compiler_params=_params("arbitrary", "arbitrary"),
        name=name,
    )(a, w, res)


def _silu_gate(cg, cv):
    return cg * (1.0 / (1.0 + jnp.exp(-cg))) * cv


def _up_conv_kernel(h_ref, w_ref, cwg_ref, cwv_ref, cbg_ref, cbv_ref, stg_ref, stv_ref,
                    act_ref, cs_ref, sg_ref, sv_ref, carry_ref, mcarry_ref,
                    *, tm, tn, n_tok_tiles, tiles_per_seq, n_batch, n_dec, dec_len, n_j):
    i = pl.program_id(0)
    j = pl.program_id(1)
    cw_refs = (cwg_ref, cwv_ref)
    cb_refs = (cbg_ref, cbv_ref)
    st_refs = (stg_ref, stv_ref)
    s_refs = (sg_ref, sv_ref)

    def split(up):
        return up[:, :tn], up[:, tn:]

    def conv(part, up, n):
        s_ref = s_refs[part]
        cw = cw_refs[part]
        s_ref[SUBLANES:SUBLANES + n, :] = up
        p1 = s_ref[SUBLANES - 1:SUBLANES - 1 + n, :]
        p2 = s_ref[SUBLANES - 2:SUBLANES - 2 + n, :]
        return cb_refs[part][...] + cw[0:1, :] * p2 + cw[1:2, :] * p1 + cw[2:3, :] * up

    @pl.when(i == 0)
    def _meta():
        ups = split(_dot(h_ref[:N_META, :], w_ref[...]))
        c = []
        for part in range(2):
            s_refs[part][:SUBLANES, :] = jnp.zeros((SUBLANES, tn), F32)
            c.append(conv(part, ups[part], N_META))
            mcarry_ref[part, j] = ups[part][N_META - SUBLANES:, :]
            cs_ref[part * n_j + j] = jnp.zeros(cs_ref.shape[1:], F32)
        act_ref[:N_META, :] = _silu_gate(c[0], c[1]).astype(BF16)

    @pl.when((i >= 1) & (i <= n_tok_tiles))
    def _prompt():
        p = i - 1
        first = (p % tiles_per_seq) == 0
        ups = split(_dot(h_ref[...], w_ref[...]))
        c = []
        for part in range(2):
            @pl.when(first)
            def _():
                s_refs[part][:SUBLANES, :] = mcarry_ref[part, j]

            @pl.when(jnp.logical_not(first))
            def _():
                s_refs[part][:SUBLANES, :] = carry_ref[part, j]

            c.append(conv(part, ups[part], tm))
            carry_ref[part, j] = ups[part][tm - SUBLANES:, :]
        act_ref[...] = _silu_gate(c[0], c[1]).astype(BF16)
        for b in range(n_batch):
            @pl.when(p == (b + 1) * tiles_per_seq - 1)
            def _():
                for part in range(2):
                    cs_ref[part * n_j + j, 2 * b:2 * b + 2, :] = ups[part][tm - 2:, :]

    @pl.when(i == n_tok_tiles + 1)
    def _sample():
        ups = split(_dot(h_ref[...], w_ref[...]))
        for s in range(n_dec):
            rows = slice(s * dec_len, (s + 1) * dec_len)
            c = []
            for part in range(2):
                up = ups[part][rows, :]
                s_refs[part][SUBLANES - 2:SUBLANES, :] = st_refs[part][s]
                c.append(conv(part, up, dec_len))
                r = 2 * (n_batch + s)
                cs_ref[part * n_j + j, r:r + 2, :] = up[dec_len - 2:, :]
            act_ref[rows, :] = _silu_gate(c[0], c[1]).astype(BF16)


def _cast_kernel(w_ref, o_ref):
    o_ref[...] = w_ref[...].astype(o_ref.dtype)


def _pair_gate_value_columns(w_up, tn=TN_FF):
    depth = w_up.shape[0]
    n_j = D_FF // tn
    return pl.pallas_call(
        _cast_kernel,
        grid=(depth, n_j, 2),
        in_specs=[pl.BlockSpec((None, D_MODEL, tn), lambda l, j, part: (l, 0, part * n_j + j))],
        out_specs=pl.BlockSpec((None, D_MODEL, tn), lambda l, j, part: (l, 0, 2 * j + part)),
        out_shape=jax.ShapeDtypeStruct(w_up.shape, BF16),
        compiler_params=_params("arbitrary", "arbitrary", "arbitrary"),
        name="pair_gate_value",
    )(w_up)


def _up_conv(h2, w_up_gv, conv_w, conv_b, state, layer, *, n_batch, seq, n_dec, dec_len, tm=TM, tn=TN_FF):
    rows = h2.shape[0]
    n_tok_tiles = n_batch * seq // tm
    assert n_dec * dec_len == tm and seq % tm == 0
    n_i = n_tok_tiles + 2
    n_j = D_FF // tn
    n_seq = n_batch + n_dec
    cs_rows = -(-2 * n_seq // SUBLANES) * SUBLANES

    def row_map(i, j):
        return ((i + n_i - 1) % n_i, 0)

    def gcol(i, j):
        return (layer, 0, j)

    def vcol(i, j):
        return (layer, 0, j + n_j)

    act, cs = pl.pallas_call(
        functools.partial(_up_conv_kernel, tm=tm, tn=tn, n_tok_tiles=n_tok_tiles, tiles_per_seq=seq // tm,
                          n_batch=n_batch, n_dec=n_dec, dec_len=dec_len, n_j=n_j),
        grid=(n_i, n_j),
        in_specs=[pl.BlockSpec((tm, D_MODEL), row_map),
                  pl.BlockSpec((None, D_MODEL, 2 * tn), lambda i, j: (layer, 0, j)),
                  pl.BlockSpec((None, CONV_W, tn), gcol), pl.BlockSpec((None, CONV_W, tn), vcol),
                  pl.BlockSpec((None, 1, tn), gcol), pl.BlockSpec((None, 1, tn), vcol),
                  pl.BlockSpec((None, n_dec, CONV_W - 1, tn), lambda i, j: (layer, 0, 0, j)),
                  pl.BlockSpec((None, n_dec, CONV_W - 1, tn), lambda i, j: (layer, 0, 0, j + n_j))],
        out_specs=[pl.BlockSpec((tm, tn), lambda i, j: ((i + n_i - 1) % n_i, j)),
                   pl.BlockSpec((2 * n_j, cs_rows, tn), lambda i, j: (0, 0, 0))],
        out_shape=[jax.ShapeDtypeStruct((rows, D_FF), BF16),
                   jax.ShapeDtypeStruct((2 * n_j, cs_rows, tn), F32)],
        scratch_shapes=[pltpu.VMEM((SUBLANES + tm, tn), F32), pltpu.VMEM((SUBLANES + tm, tn), F32),
                        pltpu.VMEM((2, n_j, SUBLANES, tn), F32), pltpu.VMEM((2, n_j, SUBLANES, tn), F32)],
        compiler_params=_params("arbitrary", "arbitrary"),
        name="up_conv",
    )(h2, w_up_gv, conv_w, conv_w, conv_b, conv_b, state, state)
    cs = cs[:, :2 * n_seq, :].reshape(2 * n_j, n_seq, 2, tn).transpose(1, 2, 0, 3).reshape(n_seq, 2, 2 * D_FF)
    return act, cs


def _rope_tables(pos):
    half = HEAD_DIM // 2
    inv = ROPE_THETA ** (-jnp.arange(half, dtype=F32) / half)
    ang = pos[:, None] * inv[None, :]
    cos = jnp.cos(ang)
    sin = jnp.sin(ang)
    return jnp.concatenate([cos, cos], axis=-1), jnp.concatenate([-sin, sin], axis=-1)


def kernel(x_prompt, x_sample, cache_k, cache_v, state_pool, state_ffn_conv, meta_tokens, norm_attn, w_in,
           lambda_q1, lambda_k1, lambda_q2, lambda_k2, subln_g, w_pool, pool_scale, w_out, norm_ffn, w_up,
           conv_w, conv_b, w_down, final_norm):
    n_batch, seq, _ = x_prompt.shape
    n_dec, dec_len, _ = x_sample.shape
    depth = w_in.shape[0]
    past = cache_k.shape[2]
    n_tok = n_batch * seq
    n_smp = n_dec * dec_len
    row_smp, row_meta = n_tok, n_tok + n_smp
    assert dec_len == CHUNK and past % CHUNK == 0 and seq % TM == 0 and n_smp % TM == 0

    x = jnp.concatenate([x_prompt.reshape(n_tok, D_MODEL), x_sample.reshape(n_smp, D_MODEL),
                         meta_tokens.astype(x_prompt.dtype)], axis=0)
    pos = jnp.concatenate([jnp.tile(N_META + jnp.arange(seq), n_batch),
                           jnp.tile(past + jnp.arange(dec_len), n_dec),
                           jnp.arange(N_META)]).astype(F32)
    cos, sin = _rope_tables(pos)

    w_in_b, w_pool_b, w_out_b, w_down_b = (w.astype(BF16) for w in (w_in, w_pool, w_out, w_down))
    n_j = D_FF // TN_FF
    w_up_gv = _pair_gate_value_columns(w_up)
    conv_b3 = conv_b.reshape(depth, 1, 2 * D_FF)
    pool_scale3 = pool_scale.reshape(depth, 1, POOL_WIDTH)
    small = tuple(a.reshape(depth, 1, HEAD_DIM) for a in (lambda_q1, lambda_k1, lambda_q2, lambda_k2))
    small += (subln_g.reshape(depth, 1, HEAD_WIDTH),)
    cache_kh = cache_k.transpose(0, 1, 3, 2, 4)
    cache_vh = cache_v.transpose(0, 1, 3, 2, 4)

    tmp = TM_DOWN
    tiles = seq // tmp
    meta_blk = row_meta // N_META
    k_col, v_col = POOL_WIDTH + ATTN_WIDTH, POOL_WIDTH + 2 * ATTN_WIDTH
    outs = {n: [] for n in ("pp", "cp", "ks", "vs", "ps", "cs")}
    cache_shape = (depth, n_batch, N_HEADS, N_META + seq, HEAD_WIDTH)
    cache_out = {"k": None, "v": None}
    for l in range(depth):
        lam_init = 0.8 - 0.6 * math.exp(-0.3 * l)
        h = _rmsnorm(x, norm_attn[l], BF16)
        (u,) = _inproj(h, w_in_b, cos, sin, l, col0=0, ncols=POOL_WIDTH, mode="u")
        (q,) = _inproj(h, w_in_b, cos, sin, l, col0=POOL_WIDTH, ncols=ATTN_WIDTH, mode="q")
        kv = {}
        for name, col0 in (("k", k_col), ("v", v_col)):
            proj = functools.partial(_inproj, h, w_in_b, cos, sin, l, col0=col0, ncols=ATTN_WIDTH, mode=name,
                                     cache_shape=cache_shape, seq=seq)
            cache_out[name], tok = proj(rows=n_tok, f32_out="heads", cache_out=cache_out[name])
            smp32, smp = proj(row0=row_smp, rows=n_smp)
            cache_out[name], meta = proj(row0=row_meta, rows=N_META, tm=N_META, f32_out="meta_heads",
                                         cache_out=cache_out[name])
            kv[name] = (tok, smp, meta)
            outs[name + "s"].append(smp32.reshape(n_dec, dec_len, N_HEADS, HEAD_WIDTH))

        y_tok = _pool_call(
            u, u, w_pool_b, pool_scale3, l, grid=(n_batch, tiles), rows=tmp,
            u_map=lambda b, t: (b * tiles + t, 0),
            hist_spec=pl.BlockSpec(
                (N_META, POOL_WIDTH),
                lambda b, t: (jnp.where(t == 0, meta_blk, (b * seq + t * tmp) // N_META - 1), 0)),
            out_rows=n_tok, out_map=lambda b, t: (b * tiles + t, 0), name="pool_prompt")
        y_smp = _pool_call(
            u, state_pool[l], w_pool_b, pool_scale3, l, grid=(n_dec,), rows=dec_len,
            u_map=lambda b: (row_smp // dec_len + b, 0),
            hist_spec=pl.BlockSpec((None, POOL_HIST, POOL_WIDTH), lambda b: (b, 0, 0)),
            out_rows=n_smp, out_map=lambda b: (b, 0), name="pool_sample")
        y_meta = _pool_call(
            u, None, w_pool_b, pool_scale3, l, grid=(1,), rows=N_META,
            u_map=lambda b: (meta_blk, 0), hist_spec=None,
            out_rows=N_META, out_map=lambda b: (0, 0), name="pool_meta")

        o_tok, o_smp, o_meta = _attention(
            q, kv["k"][0], kv["v"][0], kv["k"][1], kv["v"][1], kv["k"][2], kv["v"][2], cache_kh, cache_vh,
            small, lam_init, l,
            n_batch=n_batch, seq=seq, n_dec=n_dec, dec_len=dec_len)

        x = _out_proj((y_tok, y_smp, y_meta), (o_tok, o_smp, o_meta), w_out_b, x, l)
        h2 = _rmsnorm(x, norm_ffn[l], BF16)
        act, cs = _up_conv(h2, w_up_gv, conv_w, conv_b3, state_ffn_conv, l,
                           n_batch=n_batch, seq=seq, n_dec=n_dec, dec_len=dec_len)
        x = _mm_res(act, w_down_b, x, l, tm=TM_DOWN, tn=TN_DOWN, name="down_proj")

        outs["pp"].append(u[:n_tok].reshape(n_batch, seq, POOL_WIDTH)[:, seq - POOL_HIST:])
        outs["ps"].append(u[row_smp:row_meta].reshape(n_dec, dec_len, POOL_WIDTH)[:, dec_len - POOL_HIST:])
        outs["cp"].append(cs[:n_batch])
        outs["cs"].append(cs[n_batch:])

    y_prompt = _rmsnorm(x, final_norm, F32, row0=0, rows=n_tok).reshape(n_batch, seq, D_MODEL)
    y_sample = _rmsnorm(x, final_norm, F32, row0=row_smp, rows=n_smp).reshape(n_dec, dec_len, D_MODEL)
    st = {n: jnp.stack(vs) for n, vs in outs.items()}
    return (y_prompt, y_sample, cache_out["k"].transpose(0, 1, 3, 2, 4), cache_out["v"].transpose(0, 1, 3, 2, 4),
            st["pp"], st["cp"], st["ks"], st["vs"], st["ps"], st["cs"])
```

```python
import functools
import math

import jax
import jax.numpy as jnp
from jax import lax
from jax.experimental import pallas as pl
from jax.experimental.pallas import tpu as pltpu

F32 = jnp.float32
BF16 = jnp.bfloat16

D_MODEL = 4096
CHUNK = 64
N_META = 16
POOL_WINDOWS = (2, 4, 8, 16)
POOL_WIDTH = D_MODEL // 4
POOL_GROUP = POOL_WIDTH // len(POOL_WINDOWS)
POOL_HIST = max(POOL_WINDOWS) - 1
HEAD_DIM = 128
HEAD_WIDTH = 2 * HEAD_DIM
ATTN_WIDTH = D_MODEL - POOL_WIDTH
N_HEADS = ATTN_WIDTH // HEAD_WIDTH
D_FF = 256 * ((8 * D_MODEL // 3 + 255) // 256)
CONV_W = 3
ROPE_THETA = 10000.0
RMS_EPS = 1e-6
SUBLN_EPS = 1e-5
NEG_INF = -1e30
LOG2E = math.log2(math.e)

LANES = 128
SUBLANES = 8
VMEM_LIMIT = 56 * 1024 * 1024

TM = 1024
TM_DOWN = 512
TN_IN = 512
TN_OUT = 512
TN_FF = 256
TN_DOWN = 512
UP_CHUNKS = 2
TQ = 512


def _params(*sem, flags=None):
    return pltpu.CompilerParams(dimension_semantics=sem, vmem_limit_bytes=VMEM_LIMIT, flags=flags)


def _dot(a, b):
    return jnp.dot(a, b, preferred_element_type=F32)


def _dot_nt(a, b):
    return lax.dot_general(a, b, (((1,), (1,)), ((), ())), preferred_element_type=F32)


def _rmsnorm_rows(x, g, eps):
    ms = jnp.mean(x * x, axis=-1, keepdims=True)
    return x * lax.rsqrt(ms + eps) * g


def _rmsnorm_kernel(x_ref, g_ref, o_ref, *, n_full, tail):
    i = pl.program_id(0)

    @pl.when(i < n_full)
    def _():
        o_ref[...] = _rmsnorm_rows(x_ref[...], g_ref[...], RMS_EPS).astype(o_ref.dtype)

    if tail:
        @pl.when(i == n_full)
        def _():
            o_ref[:tail] = _rmsnorm_rows(x_ref[:tail], g_ref[...], RMS_EPS).astype(o_ref.dtype)


def _rmsnorm(x, g, out_dtype, *, row0=0, rows=None, tm=TM_DOWN):
    rows = x.shape[0] - row0 if rows is None else rows
    n_full, tail = divmod(rows, tm)
    off = row0 // tm
    assert row0 % tm == 0
    return pl.pallas_call(
        functools.partial(_rmsnorm_kernel, n_full=n_full, tail=tail),
        grid=(n_full + (1 if tail else 0),),
        in_specs=[pl.BlockSpec((tm, D_MODEL), lambda i: (i + off, 0)),
                  pl.BlockSpec((1, D_MODEL), lambda i: (0, 0))],
        out_specs=pl.BlockSpec((tm, D_MODEL), lambda i: (i, 0)),
        out_shape=jax.ShapeDtypeStruct((rows, D_MODEL), out_dtype),
        compiler_params=_params("arbitrary"),
        name="rmsnorm",
    )(x, g.reshape(1, D_MODEL))


def _inproj_kernel(x_ref, w_ref, cos_ref, sin_ref, *o_refs, mode, f32_out, tn, n_full, tail):
    i = pl.program_id(0)
    if f32_out != "rows":
        o_refs = o_refs[-2:]

    def store32(rows, cols, val):
        head, off = divmod(cols.start, HEAD_WIDTH)
        lanes = slice(off, off + cols.stop - cols.start)
        if f32_out == "heads":
            o_refs[0][head, rows, lanes] = val
        elif f32_out == "meta_heads":
            for b in range(o_refs[0].shape[0]):
                o_refs[0][b, head, rows, lanes] = val
        else:
            o_refs[0][rows, cols] = val

    def compute(rows):
        acc = _dot(x_ref[rows, :], w_ref[...])
        if mode == "u":
            o_refs[0][rows, :] = acc
            return
        if mode == "v":
            for c in range(tn // HEAD_WIDTH):
                cols = slice(c * HEAD_WIDTH, (c + 1) * HEAD_WIDTH)
                store32(rows, cols, acc[:, cols])
            o_refs[1][rows, :] = acc.astype(BF16)
            return
        cos = cos_ref[rows, :]
        sin = sin_ref[rows, :]
        for c in range(tn // HEAD_DIM):
            cols = slice(c * HEAD_DIM, (c + 1) * HEAD_DIM)
            blk = acc[:, cols]
            rot = blk * cos + pltpu.roll(blk, HEAD_DIM // 2, axis=1) * sin
            if mode == "q":
                o_refs[0][rows, cols] = (rot * (LOG2E * HEAD_DIM ** -0.5)).astype(BF16)
            else:
                store32(rows, cols, rot)
                o_refs[1][rows, cols] = rot.astype(BF16)

    @pl.when(i < n_full)
    def _():
        compute(slice(None))

    if tail:
        @pl.when(i == n_full)
        def _():
            compute(slice(0, tail))


def _inproj(h, w, cos, sin, layer, *, col0, ncols, mode, row0=0, rows=None, tm=TM, tn=TN_IN,
            f32_out="rows", cache_out=None, cache_shape=None, seq=None):
    rows = h.shape[0] - row0 if rows is None else rows
    n_full, tail = divmod(rows, tm)
    ioff = row0 // tm
    joff = col0 // tn
    assert row0 % tm == 0 and col0 % tn == 0
    out_dtypes = {"u": (F32,), "q": (BF16,), "k": (F32, BF16), "v": (F32, BF16)}[mode]
    row_spec = pl.BlockSpec((tm, tn), lambda i, j: (i, j))
    out_specs = [row_spec for _ in out_dtypes]
    out_shape = [jax.ShapeDtypeStruct((rows, ncols), dt) for dt in out_dtypes]
    in_specs = [pl.BlockSpec((tm, D_MODEL), lambda i, j: (i + ioff, 0)),
                pl.BlockSpec((None, D_MODEL, tn), lambda i, j: (layer, 0, j + joff)),
                pl.BlockSpec((tm, HEAD_DIM), lambda i, j: (i + ioff, 0)),
                pl.BlockSpec((tm, HEAD_DIM), lambda i, j: (i + ioff, 0))]
    args = [h, w, cos, sin]
    aliases = {}
    if f32_out != "rows":
        hpt = tn // HEAD_WIDTH
        n_batch = cache_shape[1]
        assert tail == 0
        if f32_out == "heads":
            tps = seq // tm
            out_specs[0] = pl.BlockSpec(
                (None, None, pl.Element(hpt), pl.Element(tm), pl.Element(HEAD_WIDTH)),
                lambda i, j: (layer, i // tps, hpt * j, pl.multiple_of(N_META + (i % tps) * tm, N_META), 0))
        else:
            assert rows == tm == N_META
            out_specs[0] = pl.BlockSpec(
                (None, pl.Element(n_batch), pl.Element(hpt), pl.Element(N_META), pl.Element(HEAD_WIDTH)),
                lambda i, j: (layer, 0, hpt * j, 0, 0))
        out_shape[0] = jax.ShapeDtypeStruct(cache_shape, F32)
        if cache_out is not None:
            in_specs.append(pl.BlockSpec(memory_space=pl.ANY))
            args.append(cache_out)
            aliases = {len(args) - 1: 0}
    return pl.pallas_call(
        functools.partial(_inproj_kernel, mode=mode, f32_out=f32_out, tn=tn, n_full=n_full, tail=tail),
        grid=(n_full + (1 if tail else 0), ncols // tn),
        in_specs=in_specs,
        out_specs=out_specs,
        out_shape=out_shape,
        input_output_aliases=aliases,
        compiler_params=_params("arbitrary", "arbitrary"),
        name="inproj_" + mode,
    )(*args)


def _pool_kernel(*refs, rows, hist_rows):
    if hist_rows:
        u_ref, hist_ref, w_ref, scale_ref, o_ref, buf_ref = refs
        buf_ref[N_META - hist_rows:N_META, :] = hist_ref[...]
    else:
        u_ref, w_ref, scale_ref, o_ref, buf_ref = refs
        buf_ref[:N_META, :] = jnp.zeros((N_META, POOL_WIDTH), F32)
    buf_ref[N_META:, :] = u_ref[...]
    for g, win in enumerate(POOL_WINDOWS):
        cols = slice(g * POOL_GROUP, (g + 1) * POOL_GROUP)
        cur = buf_ref[N_META:, cols]
        wsum = cur
        for back in range(1, win):
            wsum = wsum + buf_ref[N_META - back:N_META - back + rows, cols]
        if hist_rows:
            mean = wsum * (1.0 / win)
        else:
            t = lax.broadcasted_iota(jnp.int32, (rows, 1), 0)
            mean = wsum / jnp.minimum(t + 1, win).astype(F32)
        y = _dot((mean - cur).astype(BF16), w_ref[g])
        o_ref[:, cols] = (y * scale_ref[:, cols]).astype(BF16)


def _pool_call(u, hist, w_pool, scale, layer, *, grid, rows, u_map, hist_spec, out_rows, out_map, name):
    hist_rows = 0 if hist is None else hist_spec.block_shape[-2]
    in_specs = [pl.BlockSpec((rows, POOL_WIDTH), u_map)]
    args = [u]
    if hist is not None:
        in_specs.append(hist_spec)
        args.append(hist)
    nd = len(grid)
    in_specs += [pl.BlockSpec((None, len(POOL_WINDOWS), POOL_GROUP, POOL_GROUP), lambda *_: (layer, 0, 0, 0)),
                 pl.BlockSpec((None, 1, POOL_WIDTH), lambda *_: (layer, 0, 0))]
    args += [w_pool, scale]
    return pl.pallas_call(
        functools.partial(_pool_kernel, rows=rows, hist_rows=hist_rows),
        grid=grid,
        in_specs=in_specs,
        out_specs=pl.BlockSpec((rows, POOL_WIDTH), out_map),
        out_shape=jax.ShapeDtypeStruct((out_rows, POOL_WIDTH), BF16),
        scratch_shapes=[pltpu.VMEM((N_META + rows, POOL_WIDTH), F32)],
        compiler_params=_params(*(("arbitrary",) * nd)),
        name=name,
    )(*args)


def _lambda(lq1_ref, lk1_ref, lq2_ref, lk2_ref, lam_init):
    a = jnp.sum(lq1_ref[...] * lk1_ref[...], axis=-1, keepdims=True)
    b = jnp.sum(lq2_ref[...] * lk2_ref[...], axis=-1, keepdims=True)
    return jnp.exp(a) - jnp.exp(b) + lam_init


def _attn_finish(o1, l1, o2, l2, lam, g, lam_init):
    o = o1 / l1 - lam * (o2 / l2)
    ms = jnp.mean(o * o, axis=-1, keepdims=True)
    return (o * lax.rsqrt(ms + SUBLN_EPS) * g * (1.0 - lam_init)).astype(BF16)


def _lane_blocks(x):
    return [x[:, c * LANES:(c + 1) * LANES] for c in range(x.shape[1] // LANES)]


def _attn_chain(q, segs, m_ref, l_ref, acc_ref, m, rows):
    scores = []
    m_cur = None
    for k, _, mask in segs:
        s = _dot_nt(q, k)
        if mask is not None:
            s = jnp.where(mask, s, NEG_INF)
        scores.append(s)
        folded = functools.reduce(jnp.maximum, _lane_blocks(s)) if s.shape[1] % LANES == 0 else s
        mx = jnp.max(folded, axis=-1, keepdims=True)
        m_cur = mx if m_cur is None else jnp.maximum(m_cur, mx)
    m_old = m_ref[m, rows, :]
    m_new = jnp.maximum(m_old, m_cur)
    alpha = jnp.exp2(m_old - m_new)
    l_new = alpha * l_ref[m, rows, :]
    pv = None
    for s, (_, v, _) in zip(scores, segs):
        if s.shape[1] % LANES == 0:
            ps = [jnp.exp2(b - m_new) for b in _lane_blocks(s)]
            l_new = l_new + functools.reduce(jnp.add, ps)
            p16 = jnp.concatenate([p.astype(BF16) for p in ps], axis=1)
        else:
            p = jnp.exp2(s - m_new[:, :s.shape[1]])
            lane = lax.broadcasted_iota(jnp.int32, l_new.shape, 1)
            l_new = l_new + jnp.where(lane == 0, jnp.sum(p, axis=-1, keepdims=True), 0.0)
            p16 = p.astype(BF16)
        part = _dot(p16, v)
        pv = part if pv is None else pv + part
    l_ref[m, rows, :] = l_new
    acc_ref[m, rows, :] = jnp.concatenate([alpha, alpha], axis=1) * acc_ref[m, rows, :] + pv
    m_ref[m, rows, :] = m_new


def _attn_tok_kernel(q_ref, k_ref, v_ref, km_ref, vm_ref, lq1_ref, lk1_ref, lq2_ref, lk2_ref, g_ref,
                     o_ref, m_ref, l_ref, acc_ref, *, tq, lam_init):
    i = pl.program_id(2)
    half = tq // 2
    groups = (slice(0, half), slice(half, tq))
    maps = (slice(0, HEAD_DIM), slice(HEAD_DIM, HEAD_WIDTH))

    m_ref[...] = jnp.full(m_ref.shape, NEG_INF, F32)
    l_ref[...] = jnp.zeros(l_ref.shape, F32)
    acc_ref[...] = jnp.zeros(acc_ref.shape, F32)

    def keys(start, size, mask):
        window = pl.ds(start, size)
        return lambda cols: (k_ref[window, cols], v_ref[window, :], mask)

    def meta_keys(cols):
        return km_ref[:, cols], vm_ref[...], None

    def update(rows, segs):
        for m, cols in enumerate(maps):
            _attn_chain(q_ref[rows, cols], [seg(cols) for seg in segs], m_ref, l_ref, acc_ref, m, rows)

    def body(j, carry):
        start = pl.multiple_of(j * tq, tq)
        for rows in groups:
            update(rows, [keys(start, tq, None)])
        return carry

    lax.fori_loop(0, i, body, 0)

    qc = lax.broadcasted_iota(jnp.int32, (half, half), 0) // CHUNK
    kc = lax.broadcasted_iota(jnp.int32, (half, half), 1) // CHUNK
    mask = kc <= qc
    d0 = pl.multiple_of(i * tq, tq)
    d1 = pl.multiple_of(i * tq + half, half)
    update(groups[0], [meta_keys, keys(d0, half, mask)])
    update(groups[1], [meta_keys, keys(d0, half, None)])
    update(groups[1], [keys(d1, half, mask)])

    lam = _lambda(lq1_ref, lk1_ref, lq2_ref, lk2_ref, lam_init)
    for rows in groups:
        l1 = jnp.sum(l_ref[0, rows, :], axis=-1, keepdims=True)
        l2 = jnp.sum(l_ref[1, rows, :], axis=-1, keepdims=True)
        o_ref[rows, :] = _attn_finish(acc_ref[0, rows, :], l1, acc_ref[1, rows, :], l2, lam, g_ref[...], lam_init)


def _attn_full_kernel(*refs, has_cache, lam_init):
    if has_cache:
        q_ref, kc_ref, vc_ref, kn_ref, vn_ref, lq1_ref, lk1_ref, lq2_ref, lk2_ref, g_ref, o_ref = refs
        vc = vc_ref[...].astype(BF16)
    else:
        q_ref, kn_ref, vn_ref, lq1_ref, lk1_ref, lq2_ref, lk2_ref, g_ref, o_ref = refs
    vn = vn_ref[...]
    outs = []
    for m in range(2):
        cols = slice(m * HEAD_DIM, (m + 1) * HEAD_DIM)
        q = q_ref[:, cols]
        s_n = _dot_nt(q, kn_ref[:, cols])
        mx = jnp.max(s_n, axis=-1, keepdims=True)
        if has_cache:
            s_c = _dot_nt(q, kc_ref[:, cols].astype(BF16))
            mx = jnp.maximum(mx, jnp.max(s_c, axis=-1, keepdims=True))
        p_n = jnp.exp2(s_n - mx)
        l = jnp.sum(p_n, axis=-1, keepdims=True)
        acc = _dot(p_n.astype(BF16), vn)
        if has_cache:
            p_c = jnp.exp2(s_c - mx)
            l = l + jnp.sum(p_c, axis=-1, keepdims=True)
            acc = acc + _dot(p_c.astype(BF16), vc)
        outs.append((acc, l))
    lam = _lambda(lq1_ref, lk1_ref, lq2_ref, lk2_ref, lam_init)
    o_ref[...] = _attn_finish(outs[0][0], outs[0][1], outs[1][0], outs[1][1], lam, g_ref[...], lam_init)


def _small_specs(layer):
    vec = pl.BlockSpec((None, 1, HEAD_DIM), lambda *_: (layer, 0, 0))
    return [vec, vec, vec, vec, pl.BlockSpec((None, 1, HEAD_WIDTH), lambda *_: (layer, 0, 0))]


def _attention(q, k_tok, v_tok, k_smp, v_smp, k_meta, v_meta, cache_k, cache_v, small, lam_init, layer,
               *, n_batch, seq, n_dec, dec_len):
    n_tok = n_batch * seq
    n_smp = n_dec * dec_len
    tq = TQ
    nq = seq // tq
    q_meta_blk = (n_tok + n_smp) // N_META

    o_tok = pl.pallas_call(
        functools.partial(_attn_tok_kernel, tq=tq, lam_init=lam_init),
        grid=(n_batch, N_HEADS, nq),
        in_specs=[pl.BlockSpec((tq, HEAD_WIDTH), lambda b, h, i: (b * nq + i, h)),
                  pl.BlockSpec((seq, HEAD_WIDTH), lambda b, h, i: (b, h)),
                  pl.BlockSpec((seq, HEAD_WIDTH), lambda b, h, i: (b, h)),
                  pl.BlockSpec((N_META, HEAD_WIDTH), lambda b, h, i: (0, h)),
                  pl.BlockSpec((N_META, HEAD_WIDTH), lambda b, h, i: (0, h))] + _small_specs(layer),
        out_specs=pl.BlockSpec((tq, HEAD_WIDTH), lambda b, h, i: (b * nq + i, h)),
        out_shape=jax.ShapeDtypeStruct((n_tok, ATTN_WIDTH), BF16),
        scratch_shapes=[pltpu.VMEM((2, tq, LANES), F32), pltpu.VMEM((2, tq, LANES), F32),
                        pltpu.VMEM((2, tq, HEAD_WIDTH), F32)],
        compiler_params=_params("arbitrary", "arbitrary", "arbitrary"),
        name="attn_prompt",
    )(q, k_tok, v_tok, k_meta, v_meta, *small)

    past = cache_k.shape[3]
    smp_blk = n_tok // dec_len
    o_smp = pl.pallas_call(
        functools.partial(_attn_full_kernel, has_cache=True, lam_init=lam_init),
        grid=(n_dec, N_HEADS),
        in_specs=[pl.BlockSpec((dec_len, HEAD_WIDTH), lambda b, h: (smp_blk + b, h)),
                  pl.BlockSpec((None, None, None, past, HEAD_WIDTH), lambda b, h: (layer, b, h, 0, 0)),
                  pl.BlockSpec((None, None, None, past, HEAD_WIDTH), lambda b, h: (layer, b, h, 0, 0)),
                  pl.BlockSpec((dec_len, HEAD_WIDTH), lambda b, h: (b, h)),
                  pl.BlockSpec((dec_len, HEAD_WIDTH), lambda b, h: (b, h))] + _small_specs(layer),
        out_specs=pl.BlockSpec((dec_len, HEAD_WIDTH), lambda b, h: (b, h)),
        out_shape=jax.ShapeDtypeStruct((n_smp, ATTN_WIDTH), BF16),
        compiler_params=_params("arbitrary", "arbitrary"),
        name="attn_sample",
    )(q, cache_k, cache_v, k_smp, v_smp, *small)

    o_meta = pl.pallas_call(
        functools.partial(_attn_full_kernel, has_cache=False, lam_init=lam_init),
        grid=(N_HEADS,),
        in_specs=[pl.BlockSpec((N_META, HEAD_WIDTH), lambda h: (q_meta_blk, h)),
                  pl.BlockSpec((N_META, HEAD_WIDTH), lambda h: (0, h)),
                  pl.BlockSpec((N_META, HEAD_WIDTH), lambda h: (0, h))] + _small_specs(layer),
        out_specs=pl.BlockSpec((N_META, HEAD_WIDTH), lambda h: (0, h)),
        out_shape=jax.ShapeDtypeStruct((N_META, ATTN_WIDTH), BF16),
        compiler_params=_params("arbitrary"),
        name="attn_meta",
    )(q, k_meta, v_meta, *small)
    return o_tok, o_smp, o_meta


def _out_proj_kernel(yt_ref, ot_ref, ys_ref, os_ref, ym_ref, om_ref, w_ref, r_ref, o_ref,
                     *, n_tok_tiles, n_smp_tiles):
    i = pl.program_id(0)

    def compute(y_ref, a_ref, rows):
        o_ref[rows, :] = (r_ref[rows, :] + _dot(y_ref[rows, :], w_ref[:POOL_WIDTH, :])
                          + _dot(a_ref[rows, :], w_ref[POOL_WIDTH:, :]))

    @pl.when(i < n_tok_tiles)
    def _():
        compute(yt_ref, ot_ref, slice(None))

    @pl.when((i >= n_tok_tiles) & (i < n_tok_tiles + n_smp_tiles))
    def _():
        compute(ys_ref, os_ref, slice(None))

    @pl.when(i == n_tok_tiles + n_smp_tiles)
    def _():
        compute(ym_ref, om_ref, slice(0, N_META))


def _out_proj(y_parts, o_parts, w_out, res, layer, *, tm=TM, tn=TN_OUT):
    rows = res.shape[0]
    n_tok_tiles = y_parts[0].shape[0] // tm
    n_smp_tiles = y_parts[1].shape[0] // tm
    n_i = n_tok_tiles + n_smp_tiles + 1
    assert rows == (n_i - 1) * tm + N_META

    def tok_map(i, j):
        return (jnp.minimum(i, n_tok_tiles - 1), 0)

    def smp_map(i, j):
        return (jnp.clip(i - n_tok_tiles, 0, n_smp_tiles - 1), 0)

    def meta_map(i, j):
        return (0, 0)

    in_specs = []
    args = []
    for part, (y, o) in enumerate(zip(y_parts, o_parts)):
        row_map = (tok_map, smp_map, meta_map)[part]
        blk = N_META if part == 2 else tm
        in_specs += [pl.BlockSpec((blk, POOL_WIDTH), row_map), pl.BlockSpec((blk, ATTN_WIDTH), row_map)]
        args += [y, o]
    in_specs += [pl.BlockSpec((None, D_MODEL, tn), lambda i, j: (layer, 0, j)),
                 pl.BlockSpec((tm, tn), lambda i, j: (i, j))]
    return pl.pallas_call(
        functools.partial(_out_proj_kernel, n_tok_tiles=n_tok_tiles, n_smp_tiles=n_smp_tiles),
        grid=(n_i, D_MODEL // tn),
        in_specs=in_specs,
        out_specs=pl.BlockSpec((tm, tn), lambda i, j: (i, j)),
        out_shape=jax.ShapeDtypeStruct((rows, D_MODEL), F32),
        compiler_params=_params("arbitrary", "arbitrary"),
        name="out_proj",
    )(*args, w_out, res)


def _mm_res_kernel(a_ref, w_ref, r_ref, o_ref, *, n_full, tail):
    i = pl.program_id(0)

    @pl.when(i < n_full)
    def _():
        o_ref[...] = r_ref[...] + _dot(a_ref[...], w_ref[...])

    if tail:
        @pl.when(i == n_full)
        def _():
            o_ref[:tail] = r_ref[:tail] + _dot(a_ref[:tail], w_ref[...])


def _mm_res(a, w, res, layer, *, tm, tn, name):
    rows, kdim = a.shape
    ncols = w.shape[2]
    n_full, tail = divmod(rows, tm)
    return pl.pallas_call(
        functools.partial(_mm_res_kernel, n_full=n_full, tail=tail),
        grid=(n_full + (1 if tail else 0), ncols // tn),
        in_specs=[pl.BlockSpec((tm, kdim), lambda i, j: (i, 0)),
                  pl.BlockSpec((None, kdim, tn), lambda i, j: (layer, 0, j)),
                  pl.BlockSpec((tm, tn), lambda i, j: (i, j))],
        out_specs=pl.BlockSpec((tm, tn), lambda i, j: (i, j)),
        out_shape=jax.ShapeDtypeStruct((rows, ncols), F32),
        compiler_params=_params("arbitrary", "arbitrary"),
        name=name,
    )(a, w, res)


def _silu_gate(cg, cv):
    return cg * (1.0 / (1.0 + jnp.exp(-cg))) * cv


def _up_conv_kernel(h_ref, w_ref, cwg_ref, cwv_ref, cbg_ref, cbv_ref, stg_ref, stv_ref,
                    act_ref, cs_ref, sg_ref, sv_ref, carry_ref, mcarry_ref,
                    *, tm, tn, n_tok_tiles, tiles_per_seq, n_batch, n_dec, dec_len, n_j, n_chunks):
    i = pl.program_id(0)
    j = pl.program_id(1)
    cw_refs = (cwg_ref, cwv_ref)
    cb_refs = (cbg_ref, cbv_ref)
    st_refs = (stg_ref, stv_ref)
    s_refs = (sg_ref, sv_ref)

    def split(up):
        return up[:, :tn], up[:, tn:]

    def conv(part, up, n, base=0):
        s_ref = s_refs[part]
        cw = cw_refs[part]
        s_ref[base + SUBLANES:base + SUBLANES + n, :] = up
        p1 = s_ref[base + SUBLANES - 1:base + SUBLANES - 1 + n, :]
        p2 = s_ref[base + SUBLANES - 2:base + SUBLANES - 2 + n, :]
        return cb_refs[part][...] + cw[0:1, :] * p2 + cw[1:2, :] * p1 + cw[2:3, :] * up

    @pl.when(i == 0)
    def _meta():
        ups = split(_dot(h_ref[:N_META, :], w_ref[...]))
        c = []
        for part in range(2):
            s_refs[part][:SUBLANES, :] = jnp.zeros((SUBLANES, tn), F32)
            c.append(conv(part, ups[part], N_META))
            mcarry_ref[part, j] = ups[part][N_META - SUBLANES:, :]
            cs_ref[part * n_j + j] = jnp.zeros(cs_ref.shape[1:], F32)
        act_ref[:N_META, :] = _silu_gate(c[0], c[1]).astype(BF16)

    @pl.when((i >= 1) & (i <= n_tok_tiles))
    def _prompt():
        p = i - 1
        first = (p % tiles_per_seq) == 0
        for part in range(2):
            @pl.when(first)
            def _():
                s_refs[part][:SUBLANES, :] = mcarry_ref[part, j]

            @pl.when(jnp.logical_not(first))
            def _():
                s_refs[part][:SUBLANES, :] = carry_ref[part, j]

        cr = tm // n_chunks
        ups = None
        for chunk in range(n_chunks):
            rows = slice(chunk * cr, (chunk + 1) * cr)
            base = chunk * (SUBLANES + cr)
            prev, ups = ups, split(_dot(h_ref[rows, :], w_ref[...]))
            c = []
            for part in range(2):
                if chunk:
                    s_refs[part][base:base + SUBLANES, :] = prev[part][cr - SUBLANES:, :]
                c.append(conv(part, ups[part], cr, base))
            act_ref[rows, :] = _silu_gate(c[0], c[1]).astype(BF16)
        for part in range(2):
            carry_ref[part, j] = ups[part][cr - SUBLANES:, :]
        for b in range(n_batch):
            @pl.when(p == (b + 1) * tiles_per_seq - 1)
            def _():
                for part in range(2):
                    cs_ref[part * n_j + j, 2 * b:2 * b + 2, :] = ups[part][cr - 2:, :]

    @pl.when(i == n_tok_tiles + 1)
    def _sample():
        ups = split(_dot(h_ref[...], w_ref[...]))
        for s in range(n_dec):
            rows = slice(s * dec_len, (s + 1) * dec_len)
            c = []
            for part in range(2):
                up = ups[part][rows, :]
                s_refs[part][SUBLANES - 2:SUBLANES, :] = st_refs[part][s]
                c.append(conv(part, up, dec_len))
                r = 2 * (n_batch + s)
                cs_ref[part * n_j + j, r:r + 2, :] = up[dec_len - 2:, :]
            act_ref[rows, :] = _silu_gate(c[0], c[1]).astype(BF16)


def _cast_kernel(w_ref, o_ref):
    o_ref[...] = w_ref[...].astype(o_ref.dtype)


def _pair_gate_value_columns(w_up, tn=TN_FF):
    depth = w_up.shape[0]
    n_j = D_FF // tn
    return pl.pallas_call(
        _cast_kernel,
        grid=(depth, n_j, 2),
        in_specs=[pl.BlockSpec((None, D_MODEL, tn), lambda l, j, part: (l, 0, part * n_j + j))],
        out_specs=pl.BlockSpec((None, D_MODEL, tn), lambda l, j, part: (l, 0, 2 * j + part)),
        out_shape=jax.ShapeDtypeStruct(w_up.shape, BF16),
        compiler_params=_params("arbitrary", "arbitrary", "arbitrary"),
        name="pair_gate_value",
    )(w_up)


def _up_conv(h2, w_up_gv, conv_w, conv_b, state, layer, *, n_batch, seq, n_dec, dec_len, tm=TM, tn=TN_FF):
    rows = h2.shape[0]
    n_tok_tiles = n_batch * seq // tm
    assert n_dec * dec_len == tm and seq % tm == 0
    n_i = n_tok_tiles + 2
    n_j = D_FF // tn
    n_seq = n_batch + n_dec
    cs_rows = -(-2 * n_seq // SUBLANES) * SUBLANES

    def row_map(i, j):
        return ((i + n_i - 1) % n_i, 0)

    def gcol(i, j):
        return (layer, 0, j)

    def vcol(i, j):
        return (layer, 0, j + n_j)

    act, cs = pl.pallas_call(
        functools.partial(_up_conv_kernel, tm=tm, tn=tn, n_tok_tiles=n_tok_tiles, tiles_per_seq=seq // tm,
                          n_batch=n_batch, n_dec=n_dec, dec_len=dec_len, n_j=n_j, n_chunks=UP_CHUNKS),
        grid=(n_i, n_j),
        in_specs=[pl.BlockSpec((tm, D_MODEL), row_map),
                  pl.BlockSpec((None, D_MODEL, 2 * tn), lambda i, j: (layer, 0, j)),
                  pl.BlockSpec((None, CONV_W, tn), gcol), pl.BlockSpec((None, CONV_W, tn), vcol),
                  pl.BlockSpec((None, 1, tn), gcol), pl.BlockSpec((None, 1, tn), vcol),
                  pl.BlockSpec((None, n_dec, CONV_W - 1, tn), lambda i, j: (layer, 0, 0, j)),
                  pl.BlockSpec((None, n_dec, CONV_W - 1, tn), lambda i, j: (layer, 0, 0, j + n_j))],
        out_specs=[pl.BlockSpec((tm, tn), lambda i, j: ((i + n_i - 1) % n_i, j)),
                   pl.BlockSpec((2 * n_j, cs_rows, tn), lambda i, j: (0, 0, 0))],
        out_shape=[jax.ShapeDtypeStruct((rows, D_FF), BF16),
                   jax.ShapeDtypeStruct((2 * n_j, cs_rows, tn), F32)],
        scratch_shapes=[pltpu.VMEM((UP_CHUNKS * SUBLANES + tm, tn), F32),
                        pltpu.VMEM((UP_CHUNKS * SUBLANES + tm, tn), F32),
                        pltpu.VMEM((2, n_j, SUBLANES, tn), F32), pltpu.VMEM((2, n_j, SUBLANES, tn), F32)],
        compiler_params=_params("arbitrary", "arbitrary"),
        name="up_conv",
    )(h2, w_up_gv, conv_w, conv_w, conv_b, conv_b, state, state)
    cs = cs[:, :2 * n_seq, :].reshape(2 * n_j, n_seq, 2, tn).transpose(1, 2, 0, 3).reshape(n_seq, 2, 2 * D_FF)
    return act, cs


def _rope_tables(pos):
    half = HEAD_DIM // 2
    inv = ROPE_THETA ** (-jnp.arange(half, dtype=F32) / half)
    ang = pos[:, None] * inv[None, :]
    cos = jnp.cos(ang)
    sin = jnp.sin(ang)
    return jnp.concatenate([cos, cos], axis=-1), jnp.concatenate([-sin, sin], axis=-1)


def kernel(x_prompt, x_sample, cache_k, cache_v, state_pool, state_ffn_conv, meta_tokens, norm_attn, w_in,
           lambda_q1, lambda_k1, lambda_q2, lambda_k2, subln_g, w_pool, pool_scale, w_out, norm_ffn, w_up,
           conv_w, conv_b, w_down, final_norm):
    n_batch, seq, _ = x_prompt.shape
    n_dec, dec_len, _ = x_sample.shape
    depth = w_in.shape[0]
    past = cache_k.shape[2]
    n_tok = n_batch * seq
    n_smp = n_dec * dec_len
    row_smp, row_meta = n_tok, n_tok + n_smp
    assert dec_len == CHUNK and past % CHUNK == 0 and seq % TM == 0 and n_smp % TM == 0

    x = jnp.concatenate([x_prompt.reshape(n_tok, D_MODEL), x_sample.reshape(n_smp, D_MODEL),
                         meta_tokens.astype(x_prompt.dtype)], axis=0)
    pos = jnp.concatenate([jnp.tile(N_META + jnp.arange(seq), n_batch),
                           jnp.tile(past + jnp.arange(dec_len), n_dec),
                           jnp.arange(N_META)]).astype(F32)
    cos, sin = _rope_tables(pos)

    w_in_b, w_pool_b, w_out_b, w_down_b = (w.astype(BF16) for w in (w_in, w_pool, w_out, w_down))
    n_j = D_FF // TN_FF
    w_up_gv = _pair_gate_value_columns(w_up)
    conv_b3 = conv_b.reshape(depth, 1, 2 * D_FF)
    pool_scale3 = pool_scale.reshape(depth, 1, POOL_WIDTH)
    small = tuple(a.reshape(depth, 1, HEAD_DIM) for a in (lambda_q1, lambda_k1, lambda_q2, lambda_k2))
    small += (subln_g.reshape(depth, 1, HEAD_WIDTH),)
    cache_kh = cache_k.transpose(0, 1, 3, 2, 4)
    cache_vh = cache_v.transpose(0, 1, 3, 2, 4)

    tmp = TM_DOWN
    tiles = seq // tmp
    meta_blk = row_meta // N_META
    k_col, v_col = POOL_WIDTH + ATTN_WIDTH, POOL_WIDTH + 2 * ATTN_WIDTH
    outs = {n: [] for n in ("pp", "cp", "ks", "vs", "ps", "cs")}
    cache_shape = (depth, n_batch, N_HEADS, N_META + seq, HEAD_WIDTH)
    cache_out = {"k": None, "v": None}
    for l in range(depth):
        lam_init = 0.8 - 0.6 * math.exp(-0.3 * l)
        h = _rmsnorm(x, norm_attn[l], BF16)
        (u,) = _inproj(h, w_in_b, cos, sin, l, col0=0, ncols=POOL_WIDTH, mode="u")
        (q,) = _inproj(h, w_in_b, cos, sin, l, col0=POOL_WIDTH, ncols=ATTN_WIDTH, mode="q")
        kv = {}
        for name, col0 in (("k", k_col), ("v", v_col)):
            proj = functools.partial(_inproj, h, w_in_b, cos, sin, l, col0=col0, ncols=ATTN_WIDTH, mode=name,
                                     cache_shape=cache_shape, seq=seq)
            cache_out[name], tok = proj(rows=n_tok, f32_out="heads", cache_out=cache_out[name])
            smp32, smp = proj(row0=row_smp, rows=n_smp)
            cache_out[name], meta = proj(row0=row_meta, rows=N_META, tm=N_META, f32_out="meta_heads",
                                         cache_out=cache_out[name])
            kv[name] = (tok, smp, meta)
            outs[name + "s"].append(smp32.reshape(n_dec, dec_len, N_HEADS, HEAD_WIDTH))

        y_tok = _pool_call(
            u, u, w_pool_b, pool_scale3, l, grid=(n_batch, tiles), rows=tmp,
            u_map=lambda b, t: (b * tiles + t, 0),
            hist_spec=pl.BlockSpec(
                (N_META, POOL_WIDTH),
                lambda b, t: (jnp.where(t == 0, meta_blk, (b * seq + t * tmp) // N_META - 1), 0)),
            out_rows=n_tok, out_map=lambda b, t: (b * tiles + t, 0), name="pool_prompt")
        y_smp = _pool_call(
            u, state_pool[l], w_pool_b, pool_scale3, l, grid=(n_dec,), rows=dec_len,
            u_map=lambda b: (row_smp // dec_len + b, 0),
            hist_spec=pl.BlockSpec((None, POOL_HIST, POOL_WIDTH), lambda b: (b, 0, 0)),
            out_rows=n_smp, out_map=lambda b: (b, 0), name="pool_sample")
        y_meta = _pool_call(
            u, None, w_pool_b, pool_scale3, l, grid=(1,), rows=N_META,
            u_map=lambda b: (meta_blk, 0), hist_spec=None,
            out_rows=N_META, out_map=lambda b: (0, 0), name="pool_meta")

        o_tok, o_smp, o_meta = _attention(
            q, kv["k"][0], kv["v"][0], kv["k"][1], kv["v"][1], kv["k"][2], kv["v"][2], cache_kh, cache_vh,
            small, lam_init, l,
            n_batch=n_batch, seq=seq, n_dec=n_dec, dec_len=dec_len)

        x = _out_proj((y_tok, y_smp, y_meta), (o_tok, o_smp, o_meta), w_out_b, x, l)
        h2 = _rmsnorm(x, norm_ffn[l], BF16)
        act, cs = _up_conv(h2, w_up_gv, conv_w, conv_b3, state_ffn_conv, l,
                           n_batch=n_batch, seq=seq, n_dec=n_dec, dec_len=dec_len)
        x = _mm_res(act, w_down_b, x, l, tm=TM_DOWN, tn=TN_DOWN, name="down_proj")

        outs["pp"].append(u[:n_tok].reshape(n_batch, seq, POOL_WIDTH)[:, seq - POOL_HIST:])
        outs["ps"].append(u[row_smp:row_meta].reshape(n_dec, dec_len, POOL_WIDTH)[:, dec_len - POOL_HIST:])
        outs["cp"].append(cs[:n_batch])
        outs["cs"].append(cs[n_batch:])

    y_prompt = _rmsnorm(x, final_norm, F32, row0=0, rows=n_tok).reshape(n_batch, seq, D_MODEL)
    y_sample = _rmsnorm(x, final_norm, F32, row0=row_smp, rows=n_smp).reshape(n_dec, dec_len, D_MODEL)
    st = {n: jnp.stack(vs) for n, vs in outs.items()}
    return (y_prompt, y_sample, cache_out["k"].transpose(0, 1, 3, 2, 4), cache_out["v"].transpose(0, 1, 3, 2, 4),
            st["pp"], st["cp"], st["ks"], st["vs"], st["ps"], st["cs"])
```

```python
import functools
import math

import jax
import jax.numpy as jnp
from jax import lax
from jax.experimental import pallas as pl
from jax.experimental.pallas import tpu as pltpu

F32 = jnp.float32
BF16 = jnp.bfloat16

D_MODEL = 4096
CHUNK = 64
N_META = 16
POOL_WINDOWS = (2, 4, 8, 16)
POOL_WIDTH = D_MODEL // 4
POOL_GROUP = POOL_WIDTH // len(POOL_WINDOWS)
POOL_HIST = max(POOL_WINDOWS) - 1
HEAD_DIM = 128
HEAD_WIDTH = 2 * HEAD_DIM
ATTN_WIDTH = D_MODEL - POOL_WIDTH
N_HEADS = ATTN_WIDTH // HEAD_WIDTH
D_FF = 256 * ((8 * D_MODEL // 3 + 255) // 256)
CONV_W = 3
ROPE_THETA = 10000.0
RMS_EPS = 1e-6
SUBLN_EPS = 1e-5
NEG_INF = -1e30
LOG2E = math.log2(math.e)

LANES = 128
SUBLANES = 8
VMEM_LIMIT = 56 * 1024 * 1024

TM = 1024
TM_DOWN = 512
TN_IN = 512
TN_OUT = 512
TN_FF = 256
TN_DOWN = 512
UP_CHUNKS = 2
TQ = 512


def _params(*sem, flags=None):
    return pltpu.CompilerParams(dimension_semantics=sem, vmem_limit_bytes=VMEM_LIMIT, flags=flags)


def _dot(a, b):
    return jnp.dot(a, b, preferred_element_type=F32)


def _dot_nt(a, b):
    return lax.dot_general(a, b, (((1,), (1,)), ((), ())), preferred_element_type=F32)


def _rmsnorm_rows(x, g, eps):
    ms = jnp.mean(x * x, axis=-1, keepdims=True)
    return x * lax.rsqrt(ms + eps) * g


def _rmsnorm_kernel(x_ref, g_ref, o_ref, *, n_full, tail):
    i = pl.program_id(0)

    @pl.when(i < n_full)
    def _():
        o_ref[...] = _rmsnorm_rows(x_ref[...], g_ref[...], RMS_EPS).astype(o_ref.dtype)

    if tail:
        @pl.when(i == n_full)
        def _():
            o_ref[:tail] = _rmsnorm_rows(x_ref[:tail], g_ref[...], RMS_EPS).astype(o_ref.dtype)


def _rmsnorm(x, g, out_dtype, *, row0=0, rows=None, tm=TM_DOWN):
    rows = x.shape[0] - row0 if rows is None else rows
    n_full, tail = divmod(rows, tm)
    off = row0 // tm
    assert row0 % tm == 0
    return pl.pallas_call(
        functools.partial(_rmsnorm_kernel, n_full=n_full, tail=tail),
        grid=(n_full + (1 if tail else 0),),
        in_specs=[pl.BlockSpec((tm, D_MODEL), lambda i: (i + off, 0)),
                  pl.BlockSpec((1, D_MODEL), lambda i: (0, 0))],
        out_specs=pl.BlockSpec((tm, D_MODEL), lambda i: (i, 0)),
        out_shape=jax.ShapeDtypeStruct((rows, D_MODEL), out_dtype),
        compiler_params=_params("arbitrary"),
        name="rmsnorm",
    )(x, g.reshape(1, D_MODEL))


def _inproj_kernel(x_ref, w_ref, cos_ref, sin_ref, *o_refs, mode, f32_out, tn, n_full, tail):
    i = pl.program_id(1)
    wb_ref = o_refs[-1]
    o_refs = o_refs[:-1]
    if f32_out != "rows":
        o_refs = o_refs[-2:]

    @pl.when(i == 0)
    def _():
        wb_ref[...] = w_ref[...].astype(BF16)

    def store32(rows, cols, val):
        head, off = divmod(cols.start, HEAD_WIDTH)
        lanes = slice(off, off + cols.stop - cols.start)
        if f32_out == "heads":
            o_refs[0][head, rows, lanes] = val
        elif f32_out == "meta_heads":
            for b in range(o_refs[0].shape[0]):
                o_refs[0][b, head, rows, lanes] = val
        else:
            o_refs[0][rows, cols] = val

    def compute(rows):
        acc = _dot(x_ref[rows, :], wb_ref[...])
        if mode == "u":
            o_refs[0][rows, :] = acc
            return
        if mode == "v":
            for c in range(tn // HEAD_WIDTH):
                cols = slice(c * HEAD_WIDTH, (c + 1) * HEAD_WIDTH)
                store32(rows, cols, acc[:, cols])
            o_refs[1][rows, :] = acc.astype(BF16)
            return
        cos = cos_ref[rows, :]
        sin = sin_ref[rows, :]
        for c in range(tn // HEAD_DIM):
            cols = slice(c * HEAD_DIM, (c + 1) * HEAD_DIM)
            blk = acc[:, cols]
            rot = blk * cos + pltpu.roll(blk, HEAD_DIM // 2, axis=1) * sin
            if mode == "q":
                o_refs[0][rows, cols] = (rot * (LOG2E * HEAD_DIM ** -0.5)).astype(BF16)
            else:
                store32(rows, cols, rot)
                o_refs[1][rows, cols] = rot.astype(BF16)

    @pl.when(i < n_full)
    def _():
        compute(slice(None))

    if tail:
        @pl.when(i == n_full)
        def _():
            compute(slice(0, tail))


def _inproj(h, w, cos, sin, layer, *, col0, ncols, mode, row0=0, rows=None, tm=TM, tn=TN_IN,
            f32_out="rows", cache_out=None, cache_shape=None, seq=None):
    rows = h.shape[0] - row0 if rows is None else rows
    n_full, tail = divmod(rows, tm)
    ioff = row0 // tm
    joff = col0 // tn
    assert row0 % tm == 0 and col0 % tn == 0
    out_dtypes = {"u": (F32,), "q": (BF16,), "k": (F32, BF16), "v": (F32, BF16)}[mode]
    row_spec = pl.BlockSpec((tm, tn), lambda i, j: (i, j))
    out_specs = [row_spec for _ in out_dtypes]
    out_shape = [jax.ShapeDtypeStruct((rows, ncols), dt) for dt in out_dtypes]
    in_specs = [pl.BlockSpec((tm, D_MODEL), lambda i, j: (i + ioff, 0)),
                pl.BlockSpec((None, D_MODEL, tn), lambda i, j: (layer, 0, j + joff)),
                pl.BlockSpec((tm, HEAD_DIM), lambda i, j: (i + ioff, 0)),
                pl.BlockSpec((tm, HEAD_DIM), lambda i, j: (i + ioff, 0))]
    args = [h, w, cos, sin]
    aliases = {}
    if f32_out != "rows":
        hpt = tn // HEAD_WIDTH
        n_batch = cache_shape[1]
        assert tail == 0
        if f32_out == "heads":
            tps = seq // tm
            out_specs[0] = pl.BlockSpec(
                (None, None, pl.Element(hpt), pl.Element(tm), pl.Element(HEAD_WIDTH)),
                lambda i, j: (layer, i // tps, hpt * j, pl.multiple_of(N_META + (i % tps) * tm, N_META), 0))
        else:
            assert rows == tm == N_META
            out_specs[0] = pl.BlockSpec(
                (None, pl.Element(n_batch), pl.Element(hpt), pl.Element(N_META), pl.Element(HEAD_WIDTH)),
                lambda i, j: (layer, 0, hpt * j, 0, 0))
        out_shape[0] = jax.ShapeDtypeStruct(cache_shape, F32)
        if cache_out is not None:
            in_specs.append(pl.BlockSpec(memory_space=pl.ANY))
            args.append(cache_out)
            aliases = {len(args) - 1: 0}
    def columns_outer(spec):
        if spec.index_map is None:
            return spec
        return pl.BlockSpec(spec.block_shape, lambda j, i, f=spec.index_map: f(i, j))

    return pl.pallas_call(
        functools.partial(_inproj_kernel, mode=mode, f32_out=f32_out, tn=tn, n_full=n_full, tail=tail),
        grid=(ncols // tn, n_full + (1 if tail else 0)),
        in_specs=[columns_outer(s) for s in in_specs],
        out_specs=[columns_outer(s) for s in out_specs],
        out_shape=out_shape,
        scratch_shapes=[pltpu.VMEM((D_MODEL, tn), BF16)],
        input_output_aliases=aliases,
        compiler_params=_params("arbitrary", "arbitrary"),
        name="inproj_" + mode,
    )(*args)


def _pool_kernel(*refs, rows, hist_rows):
    if hist_rows:
        u_ref, hist_ref, w_ref, scale_ref, o_ref, buf_ref = refs
        buf_ref[N_META - hist_rows:N_META, :] = hist_ref[...]
    else:
        u_ref, w_ref, scale_ref, o_ref, buf_ref = refs
        buf_ref[:N_META, :] = jnp.zeros((N_META, POOL_WIDTH), F32)
    buf_ref[N_META:, :] = u_ref[...]
    for g, win in enumerate(POOL_WINDOWS):
        cols = slice(g * POOL_GROUP, (g + 1) * POOL_GROUP)
        cur = buf_ref[N_META:, cols]
        wsum = cur
        for back in range(1, win):
            wsum = wsum + buf_ref[N_META - back:N_META - back + rows, cols]
        if hist_rows:
            mean = wsum * (1.0 / win)
        else:
            t = lax.broadcasted_iota(jnp.int32, (rows, 1), 0)
            mean = wsum / jnp.minimum(t + 1, win).astype(F32)
        y = _dot((mean - cur).astype(BF16), w_ref[g])
        o_ref[:, cols] = (y * scale_ref[:, cols]).astype(BF16)


def _pool_call(u, hist, w_pool, scale, layer, *, grid, rows, u_map, hist_spec, out_rows, out_map, name):
    hist_rows = 0 if hist is None else hist_spec.block_shape[-2]
    in_specs = [pl.BlockSpec((rows, POOL_WIDTH), u_map)]
    args = [u]
    if hist is not None:
        in_specs.append(hist_spec)
        args.append(hist)
    nd = len(grid)
    in_specs += [pl.BlockSpec((None, len(POOL_WINDOWS), POOL_GROUP, POOL_GROUP), lambda *_: (layer, 0, 0, 0)),
                 pl.BlockSpec((None, 1, POOL_WIDTH), lambda *_: (layer, 0, 0))]
    args += [w_pool, scale]
    return pl.pallas_call(
        functools.partial(_pool_kernel, rows=rows, hist_rows=hist_rows),
        grid=grid,
        in_specs=in_specs,
        out_specs=pl.BlockSpec((rows, POOL_WIDTH), out_map),
        out_shape=jax.ShapeDtypeStruct((out_rows, POOL_WIDTH), BF16),
        scratch_shapes=[pltpu.VMEM((N_META + rows, POOL_WIDTH), F32)],
        compiler_params=_params(*(("arbitrary",) * nd)),
        name=name,
    )(*args)


def _lambda(lq1_ref, lk1_ref, lq2_ref, lk2_ref, lam_init):
    a = jnp.sum(lq1_ref[...] * lk1_ref[...], axis=-1, keepdims=True)
    b = jnp.sum(lq2_ref[...] * lk2_ref[...], axis=-1, keepdims=True)
    return jnp.exp(a) - jnp.exp(b) + lam_init


def _attn_finish(o1, l1, o2, l2, lam, g, lam_init):
    o = o1 / l1 - lam * (o2 / l2)
    ms = jnp.mean(o * o, axis=-1, keepdims=True)
    return (o * lax.rsqrt(ms + SUBLN_EPS) * g * (1.0 - lam_init)).astype(BF16)


def _lane_blocks(x):
    return [x[:, c * LANES:(c + 1) * LANES] for c in range(x.shape[1] // LANES)]


def _attn_chain(q, segs, m_ref, l_ref, acc_ref, m, rows):
    scores = []
    m_cur = None
    for k, _, mask in segs:
        s = _dot_nt(q, k)
        if mask is not None:
            s = jnp.where(mask, s, NEG_INF)
        scores.append(s)
        folded = functools.reduce(jnp.maximum, _lane_blocks(s)) if s.shape[1] % LANES == 0 else s
        mx = jnp.max(folded, axis=-1, keepdims=True)
        m_cur = mx if m_cur is None else jnp.maximum(m_cur, mx)
    m_old = m_ref[m, rows, :]
    m_new = jnp.maximum(m_old, m_cur)
    alpha = jnp.exp2(m_old - m_new)
    l_new = alpha * l_ref[m, rows, :]
    pv = None
    for s, (_, v, _) in zip(scores, segs):
        if s.shape[1] % LANES == 0:
            ps = [jnp.exp2(b - m_new) for b in _lane_blocks(s)]
            l_new = l_new + functools.reduce(jnp.add, ps)
            p16 = jnp.concatenate([p.astype(BF16) for p in ps], axis=1)
        else:
            p = jnp.exp2(s - m_new[:, :s.shape[1]])
            lane = lax.broadcasted_iota(jnp.int32, l_new.shape, 1)
            l_new = l_new + jnp.where(lane == 0, jnp.sum(p, axis=-1, keepdims=True), 0.0)
            p16 = p.astype(BF16)
        part = _dot(p16, v)
        pv = part if pv is None else pv + part
    l_ref[m, rows, :] = l_new
    acc_ref[m, rows, :] = jnp.concatenate([alpha, alpha], axis=1) * acc_ref[m, rows, :] + pv
    m_ref[m, rows, :] = m_new


def _attn_tok_kernel(q_ref, k_ref, v_ref, km_ref, vm_ref, lq1_ref, lk1_ref, lq2_ref, lk2_ref, g_ref,
                     o_ref, m_ref, l_ref, acc_ref, *, tq, lam_init):
    i = pl.program_id(2)
    half = tq // 2
    groups = (slice(0, half), slice(half, tq))
    maps = (slice(0, HEAD_DIM), slice(HEAD_DIM, HEAD_WIDTH))

    m_ref[...] = jnp.full(m_ref.shape, NEG_INF, F32)
    l_ref[...] = jnp.zeros(l_ref.shape, F32)
    acc_ref[...] = jnp.zeros(acc_ref.shape, F32)

    def keys(start, size, mask):
        window = pl.ds(start, size)
        return lambda cols: (k_ref[window, cols], v_ref[window, :], mask)

    def meta_keys(cols):
        return km_ref[:, cols], vm_ref[...], None

    def update(rows, segs):
        for m, cols in enumerate(maps):
            _attn_chain(q_ref[rows, cols], [seg(cols) for seg in segs], m_ref, l_ref, acc_ref, m, rows)

    def body(j, carry):
        start = pl.multiple_of(j * tq, tq)
        for rows in groups:
            update(rows, [keys(start, tq, None)])
        return carry

    lax.fori_loop(0, i, body, 0)

    qc = lax.broadcasted_iota(jnp.int32, (half, half), 0) // CHUNK
    kc = lax.broadcasted_iota(jnp.int32, (half, half), 1) // CHUNK
    mask = kc <= qc
    d0 = pl.multiple_of(i * tq, tq)
    d1 = pl.multiple_of(i * tq + half, half)
    update(groups[0], [meta_keys, keys(d0, half, mask)])
    update(groups[1], [meta_keys, keys(d0, half, None)])
    update(groups[1], [keys(d1, half, mask)])

    lam = _lambda(lq1_ref, lk1_ref, lq2_ref, lk2_ref, lam_init)
    for rows in groups:
        l1 = jnp.sum(l_ref[0, rows, :], axis=-1, keepdims=True)
        l2 = jnp.sum(l_ref[1, rows, :], axis=-1, keepdims=True)
        o_ref[rows, :] = _attn_finish(acc_ref[0, rows, :], l1, acc_ref[1, rows, :], l2, lam, g_ref[...], lam_init)


def _attn_full_kernel(*refs, has_cache, lam_init):
    if has_cache:
        q_ref, kc_ref, vc_ref, kn_ref, vn_ref, lq1_ref, lk1_ref, lq2_ref, lk2_ref, g_ref, o_ref = refs
        vc = vc_ref[...].astype(BF16)
    else:
        q_ref, kn_ref, vn_ref, lq1_ref, lk1_ref, lq2_ref, lk2_ref, g_ref, o_ref = refs
    vn = vn_ref[...]
    outs = []
    for m in range(2):
        cols = slice(m * HEAD_DIM, (m + 1) * HEAD_DIM)
        q = q_ref[:, cols]
        s_n = _dot_nt(q, kn_ref[:, cols])
        mx = jnp.max(s_n, axis=-1, keepdims=True)
        if has_cache:
            s_c = _dot_nt(q, kc_ref[:, cols].astype(BF16))
            mx = jnp.maximum(mx, jnp.max(s_c, axis=-1, keepdims=True))
        p_n = jnp.exp2(s_n - mx)
        l = jnp.sum(p_n, axis=-1, keepdims=True)
        acc = _dot(p_n.astype(BF16), vn)
        if has_cache:
            p_c = jnp.exp2(s_c - mx)
            l = l + jnp.sum(p_c, axis=-1, keepdims=True)
            acc = acc + _dot(p_c.astype(BF16), vc)
        outs.append((acc, l))
    lam = _lambda(lq1_ref, lk1_ref, lq2_ref, lk2_ref, lam_init)
    o_ref[...] = _attn_finish(outs[0][0], outs[0][1], outs[1][0], outs[1][1], lam, g_ref[...], lam_init)


def _small_specs(layer):
    vec = pl.BlockSpec((None, 1, HEAD_DIM), lambda *_: (layer, 0, 0))
    return [vec, vec, vec, vec, pl.BlockSpec((None, 1, HEAD_WIDTH), lambda *_: (layer, 0, 0))]


def _attention(q, k_tok, v_tok, k_smp, v_smp, k_meta, v_meta, cache_k, cache_v, small, lam_init, layer,
               *, n_batch, seq, n_dec, dec_len):
    n_tok = n_batch * seq
    n_smp = n_dec * dec_len
    tq = TQ
    nq = seq // tq
    q_meta_blk = (n_tok + n_smp) // N_META

    o_tok = pl.pallas_call(
        functools.partial(_attn_tok_kernel, tq=tq, lam_init=lam_init),
        grid=(n_batch, N_HEADS, nq),
        in_specs=[pl.BlockSpec((tq, HEAD_WIDTH), lambda b, h, i: (b * nq + i, h)),
                  pl.BlockSpec((seq, HEAD_WIDTH), lambda b, h, i: (b, h)),
                  pl.BlockSpec((seq, HEAD_WIDTH), lambda b, h, i: (b, h)),
                  pl.BlockSpec((N_META, HEAD_WIDTH), lambda b, h, i: (0, h)),
                  pl.BlockSpec((N_META, HEAD_WIDTH), lambda b, h, i: (0, h))] + _small_specs(layer),
        out_specs=pl.BlockSpec((tq, HEAD_WIDTH), lambda b, h, i: (b * nq + i, h)),
        out_shape=jax.ShapeDtypeStruct((n_tok, ATTN_WIDTH), BF16),
        scratch_shapes=[pltpu.VMEM((2, tq, LANES), F32), pltpu.VMEM((2, tq, LANES), F32),
                        pltpu.VMEM((2, tq, HEAD_WIDTH), F32)],
        compiler_params=_params("arbitrary", "arbitrary", "arbitrary"),
        name="attn_prompt",
    )(q, k_tok, v_tok, k_meta, v_meta, *small)

    past = cache_k.shape[3]
    smp_blk = n_tok // dec_len
    o_smp = pl.pallas_call(
        functools.partial(_attn_full_kernel, has_cache=True, lam_init=lam_init),
        grid=(n_dec, N_HEADS),
        in_specs=[pl.BlockSpec((dec_len, HEAD_WIDTH), lambda b, h: (smp_blk + b, h)),
                  pl.BlockSpec((None, None, None, past, HEAD_WIDTH), lambda b, h: (layer, b, h, 0, 0)),
                  pl.BlockSpec((None, None, None, past, HEAD_WIDTH), lambda b, h: (layer, b, h, 0, 0)),
                  pl.BlockSpec((dec_len, HEAD_WIDTH), lambda b, h: (b, h)),
                  pl.BlockSpec((dec_len, HEAD_WIDTH), lambda b, h: (b, h))] + _small_specs(layer),
        out_specs=pl.BlockSpec((dec_len, HEAD_WIDTH), lambda b, h: (b, h)),
        out_shape=jax.ShapeDtypeStruct((n_smp, ATTN_WIDTH), BF16),
        compiler_params=_params("arbitrary", "arbitrary"),
        name="attn_sample",
    )(q, cache_k, cache_v, k_smp, v_smp, *small)

    o_meta = pl.pallas_call(
        functools.partial(_attn_full_kernel, has_cache=False, lam_init=lam_init),
        grid=(N_HEADS,),
        in_specs=[pl.BlockSpec((N_META, HEAD_WIDTH), lambda h: (q_meta_blk, h)),
                  pl.BlockSpec((N_META, HEAD_WIDTH), lambda h: (0, h)),
                  pl.BlockSpec((N_META, HEAD_WIDTH), lambda h: (0, h))] + _small_specs(layer),
        out_specs=pl.BlockSpec((N_META, HEAD_WIDTH), lambda h: (0, h)),
        out_shape=jax.ShapeDtypeStruct((N_META, ATTN_WIDTH), BF16),
        compiler_params=_params("arbitrary"),
        name="attn_meta",
    )(q, k_meta, v_meta, *small)
    return o_tok, o_smp, o_meta


def _out_proj_kernel(yt_ref, ot_ref, ys_ref, os_ref, ym_ref, om_ref, w_ref, r_ref, o_ref,
                     *, n_tok_tiles, n_smp_tiles):
    i = pl.program_id(0)

    def compute(y_ref, a_ref, rows):
        o_ref[rows, :] = (r_ref[rows, :] + _dot(y_ref[rows, :], w_ref[:POOL_WIDTH, :])
                          + _dot(a_ref[rows, :], w_ref[POOL_WIDTH:, :]))

    @pl.when(i < n_tok_tiles)
    def _():
        compute(yt_ref, ot_ref, slice(None))

    @pl.when((i >= n_tok_tiles) & (i < n_tok_tiles + n_smp_tiles))
    def _():
        compute(ys_ref, os_ref, slice(None))

    @pl.when(i == n_tok_tiles + n_smp_tiles)
    def _():
        compute(ym_ref, om_ref, slice(0, N_META))


def _out_proj(y_parts, o_parts, w_out, res, layer, *, tm=TM, tn=TN_OUT):
    rows = res.shape[0]
    n_tok_tiles = y_parts[0].shape[0] // tm
    n_smp_tiles = y_parts[1].shape[0] // tm
    n_i = n_tok_tiles + n_smp_tiles + 1
    assert rows == (n_i - 1) * tm + N_META

    def tok_map(i, j):
        return (jnp.minimum(i, n_tok_tiles - 1), 0)

    def smp_map(i, j):
        return (jnp.clip(i - n_tok_tiles, 0, n_smp_tiles - 1), 0)

    def meta_map(i, j):
        return (0, 0)

    in_specs = []
    args = []
    for part, (y, o) in enumerate(zip(y_parts, o_parts)):
        row_map = (tok_map, smp_map, meta_map)[part]
        blk = N_META if part == 2 else tm
        in_specs += [pl.BlockSpec((blk, POOL_WIDTH), row_map), pl.BlockSpec((blk, ATTN_WIDTH), row_map)]
        args += [y, o]
    in_specs += [pl.BlockSpec((None, D_MODEL, tn), lambda i, j: (layer, 0, j)),
                 pl.BlockSpec((tm, tn), lambda i, j: (i, j))]
    return pl.pallas_call(
        functools.partial(_out_proj_kernel, n_tok_tiles=n_tok_tiles, n_smp_tiles=n_smp_tiles),
        grid=(n_i, D_MODEL // tn),
        in_specs=in_specs,
        out_specs=pl.BlockSpec((tm, tn), lambda i, j: (i, j)),
        out_shape=jax.ShapeDtypeStruct((rows, D_MODEL), F32),
        compiler_params=_params("arbitrary", "arbitrary"),
        name="out_proj",
    )(*args, w_out, res)


def _mm_res_kernel(a_ref, w_ref, r_ref, o_ref, *, n_full, tail):
    i = pl.program_id(0)

    @pl.when(i < n_full)
    def _():
        o_ref[...] = r_ref[...] + _dot(a_ref[...], w_ref[...])

    if tail:
        @pl.when(i == n_full)
        def _():
            o_ref[:tail] = r_ref[:tail] + _dot(a_ref[:tail], w_ref[...])


def _mm_res(a, w, res, layer, *, tm, tn, name):
    rows, kdim = a.shape
    ncols = w.shape[2]
    n_full, tail = divmod(rows, tm)
    return pl.pallas_call(
        functools.partial(_mm_res_kernel, n_full=n_full, tail=tail),
        grid=(n_full + (1 if tail else 0), ncols // tn),
        in_specs=[pl.BlockSpec((tm, kdim), lambda i, j: (i, 0)),
                  pl.BlockSpec((None, kdim, tn), lambda i, j: (layer, 0, j)),
                  pl.BlockSpec((tm, tn), lambda i, j: (i, j))],
        out_specs=pl.BlockSpec((tm, tn), lambda i, j: (i, j)),
        out_shape=jax.ShapeDtypeStruct((rows, ncols), F32),
        compiler_params=_params("arbitrary", "arbitrary"),
        name=name,
    )(a, w, res)


def _silu_gate(cg, cv):
    return cg * (1.0 / (1.0 + jnp.exp(-cg))) * cv


def _up_conv_kernel(h_ref, wg_ref, wv_ref, cwg_ref, cwv_ref, cbg_ref, cbv_ref, stg_ref, stv_ref,
                    act_ref, csg_ref, csv_ref, w_ref, sg_ref, sv_ref, carry_ref, mcarry_ref,
                    *, tm, tn, n_tok_tiles, tiles_per_seq, n_batch, n_dec, dec_len, n_chunks):
    i = pl.program_id(1)
    cw_refs = (cwg_ref, cwv_ref)
    cb_refs = (cbg_ref, cbv_ref)
    st_refs = (stg_ref, stv_ref)
    s_refs = (sg_ref, sv_ref)
    cs_refs = (csg_ref, csv_ref)

    def split(up):
        return up[:, :tn], up[:, tn:]

    def conv(part, up, n, base=0):
        s_ref = s_refs[part]
        cw = cw_refs[part]
        s_ref[base + SUBLANES:base + SUBLANES + n, :] = up
        p1 = s_ref[base + SUBLANES - 1:base + SUBLANES - 1 + n, :]
        p2 = s_ref[base + SUBLANES - 2:base + SUBLANES - 2 + n, :]
        return cb_refs[part][...] + cw[0:1, :] * p2 + cw[1:2, :] * p1 + cw[2:3, :] * up

    @pl.when(i == 0)
    def _meta():
        w_ref[:, :tn] = wg_ref[...].astype(BF16)
        w_ref[:, tn:] = wv_ref[...].astype(BF16)
        ups = split(_dot(h_ref[:N_META, :], w_ref[...]))
        c = []
        for part in range(2):
            s_refs[part][:SUBLANES, :] = jnp.zeros((SUBLANES, tn), F32)
            c.append(conv(part, ups[part], N_META))
            mcarry_ref[part] = ups[part][N_META - SUBLANES:, :]
            cs_refs[part][...] = jnp.zeros(cs_refs[part].shape, F32)
        act_ref[:N_META, :] = _silu_gate(c[0], c[1]).astype(BF16)

    @pl.when((i >= 1) & (i <= n_tok_tiles))
    def _prompt():
        p = i - 1
        first = (p % tiles_per_seq) == 0
        for part in range(2):
            @pl.when(first)
            def _():
                s_refs[part][:SUBLANES, :] = mcarry_ref[part]

            @pl.when(jnp.logical_not(first))
            def _():
                s_refs[part][:SUBLANES, :] = carry_ref[part]

        cr = tm // n_chunks
        ups = None
        for chunk in range(n_chunks):
            rows = slice(chunk * cr, (chunk + 1) * cr)
            base = chunk * (SUBLANES + cr)
            prev, ups = ups, split(_dot(h_ref[rows, :], w_ref[...]))
            c = []
            for part in range(2):
                if chunk:
                    s_refs[part][base:base + SUBLANES, :] = prev[part][cr - SUBLANES:, :]
                c.append(conv(part, ups[part], cr, base))
            act_ref[rows, :] = _silu_gate(c[0], c[1]).astype(BF16)
        for part in range(2):
            carry_ref[part] = ups[part][cr - SUBLANES:, :]
        for b in range(n_batch):
            @pl.when(p == (b + 1) * tiles_per_seq - 1)
            def _():
                for part in range(2):
                    cs_refs[part][2 * b:2 * b + 2, :] = ups[part][cr - 2:, :]

    @pl.when(i == n_tok_tiles + 1)
    def _sample():
        ups = split(_dot(h_ref[...], w_ref[...]))
        for s in range(n_dec):
            rows = slice(s * dec_len, (s + 1) * dec_len)
            c = []
            for part in range(2):
                up = ups[part][rows, :]
                s_refs[part][SUBLANES - 2:SUBLANES, :] = st_refs[part][s]
                c.append(conv(part, up, dec_len))
                r = 2 * (n_batch + s)
                cs_refs[part][r:r + 2, :] = up[dec_len - 2:, :]
            act_ref[rows, :] = _silu_gate(c[0], c[1]).astype(BF16)


def _up_conv(h2, w_up, conv_w, conv_b, state, layer, *, n_batch, seq, n_dec, dec_len, tm=TM, tn=TN_FF):
    rows = h2.shape[0]
    n_tok_tiles = n_batch * seq // tm
    assert n_dec * dec_len == tm and seq % tm == 0
    n_i = n_tok_tiles + 2
    n_j = D_FF // tn
    n_seq = n_batch + n_dec
    cs_rows = -(-2 * n_seq // SUBLANES) * SUBLANES

    def row_map(j, i):
        return ((i + n_i - 1) % n_i, 0)

    def gcol(j, i):
        return (layer, 0, j)

    def vcol(j, i):
        return (layer, 0, j + n_j)

    cs_spec = pl.BlockSpec((None, cs_rows, tn), lambda j, i: (j, 0, 0))
    cs_shape = jax.ShapeDtypeStruct((n_j, cs_rows, tn), F32)
    act, csg, csv = pl.pallas_call(
        functools.partial(_up_conv_kernel, tm=tm, tn=tn, n_tok_tiles=n_tok_tiles, tiles_per_seq=seq // tm,
                          n_batch=n_batch, n_dec=n_dec, dec_len=dec_len, n_chunks=UP_CHUNKS),
        grid=(n_j, n_i),
        in_specs=[pl.BlockSpec((tm, D_MODEL), row_map),
                  pl.BlockSpec((None, D_MODEL, tn), gcol), pl.BlockSpec((None, D_MODEL, tn), vcol),
                  pl.BlockSpec((None, CONV_W, tn), gcol), pl.BlockSpec((None, CONV_W, tn), vcol),
                  pl.BlockSpec((None, 1, tn), gcol), pl.BlockSpec((None, 1, tn), vcol),
                  pl.BlockSpec((None, n_dec, CONV_W - 1, tn), lambda j, i: (layer, 0, 0, j)),
                  pl.BlockSpec((None, n_dec, CONV_W - 1, tn), lambda j, i: (layer, 0, 0, j + n_j))],
        out_specs=[pl.BlockSpec((tm, tn), lambda j, i: ((i + n_i - 1) % n_i, j)), cs_spec, cs_spec],
        out_shape=[jax.ShapeDtypeStruct((rows, D_FF), BF16), cs_shape, cs_shape],
        scratch_shapes=[pltpu.VMEM((D_MODEL, 2 * tn), BF16),
                        pltpu.VMEM((UP_CHUNKS * SUBLANES + tm, tn), F32),
                        pltpu.VMEM((UP_CHUNKS * SUBLANES + tm, tn), F32),
                        pltpu.VMEM((2, SUBLANES, tn), F32), pltpu.VMEM((2, SUBLANES, tn), F32)],
        compiler_params=_params("arbitrary", "arbitrary"),
        name="up_conv",
    )(h2, w_up, w_up, conv_w, conv_w, conv_b, conv_b, state, state)
    cs = jnp.concatenate([csg, csv], axis=0)
    cs = cs[:, :2 * n_seq, :].reshape(2 * n_j, n_seq, 2, tn).transpose(1, 2, 0, 3).reshape(n_seq, 2, 2 * D_FF)
    return act, cs


def _rope_tables(pos):
    half = HEAD_DIM // 2
    inv = ROPE_THETA ** (-jnp.arange(half, dtype=F32) / half)
    ang = pos[:, None] * inv[None, :]
    cos = jnp.cos(ang)
    sin = jnp.sin(ang)
    return jnp.concatenate([cos, cos], axis=-1), jnp.concatenate([-sin, sin], axis=-1)


def kernel(x_prompt, x_sample, cache_k, cache_v, state_pool, state_ffn_conv, meta_tokens, norm_attn, w_in,
           lambda_q1, lambda_k1, lambda_q2, lambda_k2, subln_g, w_pool, pool_scale, w_out, norm_ffn, w_up,
           conv_w, conv_b, w_down, final_norm):
    n_batch, seq, _ = x_prompt.shape
    n_dec, dec_len, _ = x_sample.shape
    depth = w_in.shape[0]
    past = cache_k.shape[2]
    n_tok = n_batch * seq
    n_smp = n_dec * dec_len
    row_smp, row_meta = n_tok, n_tok + n_smp
    assert dec_len == CHUNK and past % CHUNK == 0 and seq % TM == 0 and n_smp % TM == 0

    x = jnp.concatenate([x_prompt.reshape(n_tok, D_MODEL), x_sample.reshape(n_smp, D_MODEL),
                         meta_tokens.astype(x_prompt.dtype)], axis=0)
    pos = jnp.concatenate([jnp.tile(N_META + jnp.arange(seq), n_batch),
                           jnp.tile(past + jnp.arange(dec_len), n_dec),
                           jnp.arange(N_META)]).astype(F32)
    cos, sin = _rope_tables(pos)

    w_pool_b, w_out_b, w_down_b = (w.astype(BF16) for w in (w_pool, w_out, w_down))
    conv_b3 = conv_b.reshape(depth, 1, 2 * D_FF)
    pool_scale3 = pool_scale.reshape(depth, 1, POOL_WIDTH)
    small = tuple(a.reshape(depth, 1, HEAD_DIM) for a in (lambda_q1, lambda_k1, lambda_q2, lambda_k2))
    small += (subln_g.reshape(depth, 1, HEAD_WIDTH),)
    cache_kh = cache_k.transpose(0, 1, 3, 2, 4)
    cache_vh = cache_v.transpose(0, 1, 3, 2, 4)

    tmp = TM_DOWN
    tiles = seq // tmp
    meta_blk = row_meta // N_META
    k_col, v_col = POOL_WIDTH + ATTN_WIDTH, POOL_WIDTH + 2 * ATTN_WIDTH
    outs = {n: [] for n in ("pp", "cp", "ks", "vs", "ps", "cs")}
    cache_shape = (depth, n_batch, N_HEADS, N_META + seq, HEAD_WIDTH)
    cache_out = {"k": jnp.zeros(cache_shape, F32), "v": jnp.zeros(cache_shape, F32)}
    for l in range(depth):
        lam_init = 0.8 - 0.6 * math.exp(-0.3 * l)
        h = _rmsnorm(x, norm_attn[l], BF16)
        (u,) = _inproj(h, w_in, cos, sin, l, col0=0, ncols=POOL_WIDTH, mode="u")
        (q,) = _inproj(h, w_in, cos, sin, l, col0=POOL_WIDTH, ncols=ATTN_WIDTH, mode="q")
        kv = {}
        for name, col0 in (("k", k_col), ("v", v_col)):
            proj = functools.partial(_inproj, h, w_in, cos, sin, l, col0=col0, ncols=ATTN_WIDTH, mode=name,
                                     cache_shape=cache_shape, seq=seq)
            cache_out[name], tok = proj(rows=n_tok, f32_out="heads", cache_out=cache_out[name])
            smp32, smp = proj(row0=row_smp, rows=n_smp)
            cache_out[name], meta = proj(row0=row_meta, rows=N_META, tm=N_META, f32_out="meta_heads",
                                         cache_out=cache_out[name])
            kv[name] = (tok, smp, meta)
            outs[name + "s"].append(smp32.reshape(n_dec, dec_len, N_HEADS, HEAD_WIDTH))

        y_tok = _pool_call(
            u, u, w_pool_b, pool_scale3, l, grid=(n_batch, tiles), rows=tmp,
            u_map=lambda b, t: (b * tiles + t, 0),
            hist_spec=pl.BlockSpec(
                (N_META, POOL_WIDTH),
                lambda b, t: (jnp.where(t == 0, meta_blk, (b * seq + t * tmp) // N_META - 1), 0)),
            out_rows=n_tok, out_map=lambda b, t: (b * tiles + t, 0), name="pool_prompt")
        y_smp = _pool_call(
            u, state_pool[l], w_pool_b, pool_scale3, l, grid=(n_dec,), rows=dec_len,
            u_map=lambda b: (row_smp // dec_len + b, 0),
            hist_spec=pl.BlockSpec((None, POOL_HIST, POOL_WIDTH), lambda b: (b, 0, 0)),
            out_rows=n_smp, out_map=lambda b: (b, 0), name="pool_sample")
        y_meta = _pool_call(
            u, None, w_pool_b, pool_scale3, l, grid=(1,), rows=N_META,
            u_map=lambda b: (meta_blk, 0), hist_spec=None,
            out_rows=N_META, out_map=lambda b: (0, 0), name="pool_meta")

        o_tok, o_smp, o_meta = _attention(
            q, kv["k"][0], kv["v"][0], kv["k"][1], kv["v"][1], kv["k"][2], kv["v"][2], cache_kh, cache_vh,
            small, lam_init, l,
            n_batch=n_batch, seq=seq, n_dec=n_dec, dec_len=dec_len)

        x = _out_proj((y_tok, y_smp, y_meta), (o_tok, o_smp, o_meta), w_out_b, x, l)
        h2 = _rmsnorm(x, norm_ffn[l], BF16)
        act, cs = _up_conv(h2, w_up, conv_w, conv_b3, state_ffn_conv, l,
                           n_batch=n_batch, seq=seq, n_dec=n_dec, dec_len=dec_len)
        x = _mm_res(act, w_down_b, x, l, tm=TM_DOWN, tn=TN_DOWN, name="down_proj")

        outs["pp"].append(u[:n_tok].reshape(n_batch, seq, POOL_WIDTH)[:, seq - POOL_HIST:])
        outs["ps"].append(u[row_smp:row_meta].reshape(n_dec, dec_len, POOL_WIDTH)[:, dec_len - POOL_HIST:])
        outs["cp"].append(cs[:n_batch])
        outs["cs"].append(cs[n_batch:])

    y_prompt = _rmsnorm(x, final_norm, F32, row0=0, rows=n_tok).reshape(n_batch, seq, D_MODEL)
    y_sample = _rmsnorm(x, final_norm, F32, row0=row_smp, rows=n_smp).reshape(n_dec, dec_len, D_MODEL)
    st = {n: jnp.stack(vs) for n, vs in outs.items()}
    return (y_prompt, y_sample, cache_out["k"].transpose(0, 1, 3, 2, 4), cache_out["v"].transpose(0, 1, 3, 2, 4),
            st["pp"], st["cp"], st["ks"], st["vs"], st["ps"], st["cs"])
```
